```python
import jax, jax.numpy as jnp
from jax import lax
import numpy as np

D_MODEL = 1024
BATCH = 8
SEQ = 2048
DEPTH = 4
DEC_BATCH = 128
DEC_SEQ = 4
PAST_LEN = 8192
PAGE_SIZE = 128

HEAD_DIM = 64
LRU_WIDTH = D_MODEL // 2
LRU_BLOCKS = 8
LRU_BLOCK_W = LRU_WIDTH // LRU_BLOCKS
LRU_C = 8.0
CONV_W = 4
SWA_HEADS = (D_MODEL // 2) // HEAD_DIM
SWA_KV_HEADS = 2
SWA_GROUP = SWA_HEADS // SWA_KV_HEADS
SWA_WINDOW = 128
EVEN_MIX = LRU_WIDTH + SWA_HEADS * HEAD_DIM
DIL_WINDOWS = (128, 512, 2048)
DIL_RATES = (1, 4, 16)
N_DIL = 3
DIL_HEADS = 8
ODD_IN = N_DIL * 3 * DIL_HEADS * HEAD_DIM
ODD_MIX = DIL_HEADS * HEAD_DIM
BAND_BLOCK = 128
N_EXPERTS = 64
N_EXPERT_GROUPS = 8
TOPK_GROUPS = 4
TOP_K = 8
D_EXPERT = 256
D_SHARED = 256
ROUTED_SCALE = 2.5
MOE_BLOCK = 128
LN_EPS = 1e-5
DEEPNORM_ALPHA = (2 * DEPTH) ** 0.25
DEEPNORM_BETA = (8 * DEPTH) ** -0.25
N_EVEN = (DEPTH + 1) // 2
N_ODD = DEPTH // 2

kernel_name = 'hybrid_lru_swa_dilated_moe_step'

F32 = jnp.float32


def alibi_slopes(n):
    return jnp.exp2(-8.0 * jnp.arange(1, n + 1, dtype=F32) / n)


def post_norm(x, f, g, b):
    z = DEEPNORM_ALPHA * x.astype(F32) + f.astype(F32)
    zc = z - jnp.mean(z, -1, keepdims=True)
    var = jnp.mean(zc * zc, -1, keepdims=True)
    return (zc * lax.rsqrt(var + LN_EPS) * g.astype(F32) + b.astype(F32)).astype(x.dtype)


def causal_conv(x_full, w, b):
    T = x_full.shape[1] - (CONV_W - 1)
    y = b.astype(F32)
    for j in range(CONV_W):
        y = y + w[j].astype(F32) * x_full[:, j:j + T].astype(F32)
    return y


def rg_lru(x, h0, w_r, b_r, w_i, b_i, lam):
    B, T, C = x.shape
    xb = x.reshape(B, T, LRU_BLOCKS, LRU_BLOCK_W)
    r = jax.nn.sigmoid(jnp.einsum('btnc,ncd->btnd', xb, w_r.astype(F32)).reshape(B, T, C) + b_r.astype(F32))
    i = jax.nn.sigmoid(jnp.einsum('btnc,ncd->btnd', xb, w_i.astype(F32)).reshape(B, T, C) + b_i.astype(F32))
    log_a = -LRU_C * r * jax.nn.softplus(-lam.astype(F32))
    a = jnp.exp(log_a)
    u = jnp.sqrt(-jnp.expm1(2.0 * log_a)) * (i * x)

    def combine(lhs, rhs):
        a1, b1 = lhs
        a2, b2 = rhs
        return a1 * a2, a2 * b1 + b2

    a_cum, u_cum = lax.associative_scan(combine, (a, u), axis=1)
    return a_cum * h0.astype(F32)[:, None] + u_cum


def banded_attention(q, k, v, slopes, window_idx, dist_scale):
    B, L, KV, G, Dh = q.shape
    nb = -(-L // BAND_BLOCK)
    pad = nb * BAND_BLOCK - L
    qb = jnp.pad(q, ((0, 0), (0, pad), (0, 0), (0, 0), (0, 0))).reshape(B, nb, BAND_BLOCK, KV, G, Dh)

    def with_prev_block(a):
        ap = jnp.pad(a, ((0, 0), (BAND_BLOCK, pad), (0, 0), (0, 0))).reshape(B, nb + 1, BAND_BLOCK, KV, Dh)
        return jnp.concatenate([ap[:, :-1], ap[:, 1:]], axis=2)

    kb, vb = with_prev_block(k), with_prev_block(v)
    s = jnp.einsum('bnqkgd,bnskd->bnkgqs', qb, kb, preferred_element_type=F32) * (Dh ** -0.5)
    qi = jnp.arange(BAND_BLOCK)[:, None]
    kj = jnp.arange(2 * BAND_BLOCK)[None, :]
    dist = qi + BAND_BLOCK - kj
    key_pos = jnp.arange(nb)[:, None, None] * BAND_BLOCK - BAND_BLOCK + kj[None]
    valid = (dist >= 0) & (dist <= window_idx) & (key_pos >= 0)
    s = s - slopes.astype(F32)[None, None, :, :, None, None] * (dist_scale * dist).astype(F32)
    s = jnp.where(valid[None, :, None, None], s, -jnp.inf)
    lse = jax.nn.logsumexp(s, axis=-1)
    p = jnp.exp(s - lse[..., None])
    o = jnp.einsum('bnkgqs,bnskd->bnqkgd', p, vb)
    o = o.reshape(B, nb * BAND_BLOCK, KV, G, Dh)[:, :L]
    lse = lse.transpose(0, 1, 4, 2, 3).reshape(B, nb * BAND_BLOCK, KV, G)[:, :L]
    return o, lse


def strided_window_decode(q, k_all, v_all, n_past, window, dil, slopes):
    T, Dh = q.shape[1], q.shape[-1]
    steps = jnp.arange(window // dil + 1)
    idx = n_past + jnp.arange(T)[:, None] - dil * steps[None, :]
    valid = idx >= 0
    idx = jnp.maximum(idx, 0)
    kg = k_all[:, idx]
    vg = v_all[:, idx]
    s = jnp.einsum('btkgd,btnkd->btkgn', q, kg, preferred_element_type=F32) * (Dh ** -0.5)
    s = s - slopes.astype(F32)[None, None, :, :, None] * (dil * steps).astype(F32)
    s = jnp.where(valid[None, :, None, None, :], s, -jnp.inf)
    lse = jax.nn.logsumexp(s, axis=-1)
    p = jnp.exp(s - lse[..., None])
    o = jnp.einsum('btkgn,btnkd->btkgd', p, vg)
    return o, lse


def dilated_prompt_attention(q, k, v, window, dil, slopes):
    B, L, H, Dh = q.shape
    Lf = L // dil

    def fold(a):
        return a.reshape(B, Lf, dil, H, Dh).transpose(0, 2, 1, 3, 4).reshape(B * dil, Lf, H, Dh)

    o, lse = banded_attention(fold(q)[:, :, :, None], fold(k), fold(v), slopes, window // dil, dil)
    o = o[:, :, :, 0].reshape(B, dil, Lf, H, Dh).transpose(0, 2, 1, 3, 4).reshape(B, L, H, Dh)
    lse = lse[..., 0].reshape(B, dil, Lf, H).transpose(0, 2, 1, 3).reshape(B, L, H)
    return o, lse


def even_mixer(x, w_in, conv_w, conv_b, w_rgate, b_rgate, w_igate, b_igate, lru_lambda, sinks, w_out, past):
    B, T, _ = x.shape
    proj = jnp.einsum('btd,de->bte', x, w_in)
    o1 = LRU_WIDTH
    o2 = 2 * LRU_WIDTH
    o3 = o2 + SWA_HEADS * HEAD_DIM
    o4 = o3 + SWA_KV_HEADS * HEAD_DIM
    xr, yg = proj[..., :o1], proj[..., o1:o2]
    q = proj[..., o2:o3].reshape(B, T, SWA_KV_HEADS, SWA_GROUP, HEAD_DIM)
    k = proj[..., o3:o4].reshape(B, T, SWA_KV_HEADS, HEAD_DIM)
    v = proj[..., o4:].reshape(B, T, SWA_KV_HEADS, HEAD_DIM)
    if past is None:
        h0 = jnp.zeros((B, LRU_WIDTH), x.dtype)
        conv_buf = jnp.zeros((B, CONV_W - 1, LRU_WIDTH), x.dtype)
    else:
        h0, conv_buf, kv_buf = past
    xr_full = jnp.concatenate([conv_buf.astype(xr.dtype), xr], axis=1)
    h = rg_lru(causal_conv(xr_full, conv_w, conv_b), h0, w_rgate, b_rgate, w_igate, b_igate, lru_lambda)
    a_out = (h * jax.nn.gelu(yg.astype(F32))).astype(x.dtype)
    slopes = alibi_slopes(SWA_HEADS).reshape(SWA_KV_HEADS, SWA_GROUP)
    if past is None:
        o, lse = banded_attention(q, k, v, slopes, SWA_WINDOW, 1)
        keep = min(SWA_WINDOW, T)
        new_kv = jnp.stack([k[:, T - keep:], v[:, T - keep:]], axis=2)
    else:
        n_past = kv_buf.shape[1]
        k_all = jnp.concatenate([kv_buf[:, :, 0].astype(k.dtype), k], axis=1)
        v_all = jnp.concatenate([kv_buf[:, :, 1].astype(v.dtype), v], axis=1)
        o, lse = strided_window_decode(q, k_all, v_all, n_past, SWA_WINDOW, 1, slopes)
        new_kv = jnp.stack([k_all[:, T:], v_all[:, T:]], axis=2)
    sink = sinks.astype(F32).reshape(SWA_KV_HEADS, SWA_GROUP)
    o = o * jax.nn.sigmoid(lse - sink)[..., None]
    mix = jnp.concatenate([a_out, o.reshape(B, T, SWA_HEADS * HEAD_DIM).astype(x.dtype)], axis=-1)
    out = jnp.einsum('bte,ed->btd', mix, w_out)
    return out, (h[:, -1].astype(x.dtype), xr_full[:, -(CONV_W - 1):], new_kv)


def odd_mixer(x, w_in, w_out, past):
    B, T, _ = x.shape
    proj = jnp.einsum('btd,de->bte', x, w_in).reshape(B, T, N_DIL, 3, DIL_HEADS, HEAD_DIM)
    slopes = alibi_slopes(DIL_HEADS)[:, None]
    outs, lses, new = [], [], []
    for g in range(N_DIL):
        win, dil = DIL_WINDOWS[g], DIL_RATES[g]
        q, k, v = proj[:, :, g, 0], proj[:, :, g, 1], proj[:, :, g, 2]
        if past is None:
            o, lse = dilated_prompt_attention(q, k, v, win, dil, slopes)
            keep = min(win, T)
            new.append(jnp.stack([k[:, T - keep:], v[:, T - keep:]], axis=2))
        else:
            kv_buf = past[g]
            n_past = kv_buf.shape[1]
            k_all = jnp.concatenate([kv_buf[:, :, 0].astype(k.dtype), k], axis=1)
            v_all = jnp.concatenate([kv_buf[:, :, 1].astype(v.dtype), v], axis=1)
            o, lse = strided_window_decode(q[:, :, :, None], k_all, v_all, n_past, win, dil, slopes)
            o, lse = o[:, :, :, 0], lse[..., 0]
            new.append(jnp.stack([k_all[:, T:], v_all[:, T:]], axis=2))
        outs.append(o)
        lses.append(lse)
    wts = jax.nn.softmax(jnp.stack(lses, 0), axis=0)
    o = jnp.einsum('gbth,gbthd->bthd', wts, jnp.stack(outs, 0))
    out = jnp.einsum('bte,ed->btd', o.reshape(B, T, ODD_MIX).astype(x.dtype), w_out)
    return out, new


def moe_ffn(x, w_router, router_bias, w_gate, w_up, w_down, ws_gate, ws_up, ws_down):
    B, T, D = x.shape
    xf = x.reshape(-1, D)
    N = xf.shape[0]
    scores = jax.nn.sigmoid(jnp.einsum('nd,de->ne', xf, w_router, preferred_element_type=F32))
    biased = scores + router_bias.astype(F32)
    grp = biased.reshape(N, N_EXPERT_GROUPS, N_EXPERTS // N_EXPERT_GROUPS)
    grp_score = lax.top_k(grp, 2)[0].sum(-1)
    top_grp = lax.top_k(grp_score, TOPK_GROUPS)[1]
    grp_mask = jnp.any(top_grp[..., None] == jnp.arange(N_EXPERT_GROUPS), axis=1)
    exp_mask = jnp.repeat(grp_mask, N_EXPERTS // N_EXPERT_GROUPS, axis=1)
    top_e = lax.top_k(jnp.where(exp_mask, biased, -jnp.inf), TOP_K)[1]
    gate = jnp.take_along_axis(scores, top_e, axis=1)
    gate = gate / jnp.sum(gate, -1, keepdims=True) * ROUTED_SCALE
    flat_e = top_e.reshape(-1)
    order = jnp.argsort(flat_e)
    e_sorted = flat_e[order]
    tok_sorted = order // TOP_K
    counts = jnp.bincount(flat_e, length=N_EXPERTS)
    padded = (counts + MOE_BLOCK - 1) // MOE_BLOCK * MOE_BLOCK
    start = jnp.cumsum(counts) - counts
    pend = jnp.cumsum(padded)
    pstart = pend - padded
    dest = pstart[e_sorted] + (jnp.arange(N * TOP_K) - start[e_sorted])
    n_blocks = -(-(N * TOP_K) // MOE_BLOCK) + N_EXPERTS
    x_disp = jnp.zeros((n_blocks * MOE_BLOCK, D), x.dtype).at[dest].set(xf[tok_sorted])
    block_e = jnp.minimum(jnp.searchsorted(pend, jnp.arange(n_blocks) * MOE_BLOCK, side='right'), N_EXPERTS - 1)

    def expert_block(args):
        xb, e = args
        hb = jax.nn.silu(xb @ w_gate[e]) * (xb @ w_up[e])
        return hb @ w_down[e]

    y_disp = lax.map(expert_block, (x_disp.reshape(n_blocks, MOE_BLOCK, D), block_e))
    y_sorted = y_disp.reshape(-1, D)[dest]
    gate_sorted = gate.reshape(-1)[order]
    routed = jax.ops.segment_sum(y_sorted.astype(F32) * gate_sorted[:, None], tok_sorted, num_segments=N)
    shared = (jax.nn.silu(xf @ ws_gate) * (xf @ ws_up)) @ ws_down
    return (routed + shared.astype(F32)).astype(x.dtype).reshape(B, T, D)


def setup_inputs(seed: int = 0) -> dict:
    key = jax.random.key(seed)
    ks = iter(jax.random.split(key, 40))

    def nrm(shape, scale):
        return scale * jax.random.normal(next(ks), shape, F32)

    swa_buf = min(SWA_WINDOW, PAST_LEN)
    dil_buf = [min(w, PAST_LEN) for w in DIL_WINDOWS]
    u = jax.random.uniform(next(ks), (N_EVEN, LRU_WIDTH), F32, 0.9, 0.999)
    p = u ** (1.0 / LRU_C)
    lam = jnp.log(p) - jnp.log1p(-p)
    dil_kv = lambda n: (N_ODD, DEC_BATCH, n, 2, DIL_HEADS, HEAD_DIM)
    return {
        'x_prompt': nrm((BATCH, SEQ, D_MODEL), 1.0),
        'x_sample': nrm((DEC_BATCH, DEC_SEQ, D_MODEL), 1.0),
        'state_lru_h': nrm((N_EVEN, DEC_BATCH, LRU_WIDTH), 1.0),
        'state_lru_conv': nrm((N_EVEN, DEC_BATCH, CONV_W - 1, LRU_WIDTH), 1.0),
        'cache_swa_kv': nrm((N_EVEN, DEC_BATCH, swa_buf, 2, SWA_KV_HEADS, HEAD_DIM), 1.0),
        'cache_dil1_kv': nrm(dil_kv(dil_buf[0]), 1.0),
        'cache_dil2_kv': nrm(dil_kv(dil_buf[1]), 1.0),
        'cache_dil3_kv': nrm(dil_kv(dil_buf[2]), 1.0),
        'w_in_even': nrm((N_EVEN, D_MODEL, 2 * LRU_WIDTH + (SWA_HEADS + 2 * SWA_KV_HEADS) * HEAD_DIM), D_MODEL ** -0.5),
        'conv_w': nrm((N_EVEN, CONV_W, LRU_WIDTH), CONV_W ** -0.5),
        'conv_b': nrm((N_EVEN, LRU_WIDTH), 0.01),
        'w_rgate': nrm((N_EVEN, LRU_BLOCKS, LRU_BLOCK_W, LRU_BLOCK_W), LRU_BLOCK_W ** -0.5),
        'b_rgate': nrm((N_EVEN, LRU_WIDTH), 0.01),
        'w_igate': nrm((N_EVEN, LRU_BLOCKS, LRU_BLOCK_W, LRU_BLOCK_W), LRU_BLOCK_W ** -0.5),
        'b_igate': nrm((N_EVEN, LRU_WIDTH), 0.01),
        'lru_lambda': lam,
        'swa_sinks': nrm((N_EVEN, SWA_HEADS), 1.0),
        'w_out_even': nrm((N_EVEN, EVEN_MIX, D_MODEL), EVEN_MIX ** -0.5 * DEEPNORM_BETA),
        'w_in_odd': nrm((N_ODD, D_MODEL, ODD_IN), D_MODEL ** -0.5),
        'w_out_odd': nrm((N_ODD, ODD_MIX, D_MODEL), ODD_MIX ** -0.5 * DEEPNORM_BETA),
        'ln1_g': 1.0 + nrm((DEPTH, D_MODEL), 0.02),
        'ln1_b': nrm((DEPTH, D_MODEL), 0.02),
        'ln2_g': 1.0 + nrm((DEPTH, D_MODEL), 0.02),
        'ln2_b': nrm((DEPTH, D_MODEL), 0.02),
        'w_router': nrm((DEPTH, D_MODEL, N_EXPERTS), D_MODEL ** -0.5),
        'router_bias': nrm((DEPTH, N_EXPERTS), 0.01),
        'w_exp_gate': nrm((DEPTH, N_EXPERTS, D_MODEL, D_EXPERT), D_MODEL ** -0.5),
        'w_exp_up': nrm((DEPTH, N_EXPERTS, D_MODEL, D_EXPERT), D_MODEL ** -0.5),
        'w_exp_down': nrm((DEPTH, N_EXPERTS, D_EXPERT, D_MODEL), D_EXPERT ** -0.5 * DEEPNORM_BETA),
        'w_sh_gate': nrm((DEPTH, D_MODEL, D_SHARED), D_MODEL ** -0.5),
        'w_sh_up': nrm((DEPTH, D_MODEL, D_SHARED), D_MODEL ** -0.5),
        'w_sh_down': nrm((DEPTH, D_SHARED, D_MODEL), D_SHARED ** -0.5 * DEEPNORM_BETA),
    }


def reference(x_prompt, x_sample, state_lru_h, state_lru_conv, cache_swa_kv, cache_dil1_kv, cache_dil2_kv,
              cache_dil3_kv, w_in_even, conv_w, conv_b, w_rgate, b_rgate, w_igate, b_igate, lru_lambda,
              swa_sinks, w_out_even, w_in_odd, w_out_odd, ln1_g, ln1_b, ln2_g, ln2_b, w_router, router_bias,
              w_exp_gate, w_exp_up, w_exp_down, w_sh_gate, w_sh_up, w_sh_down):
    xp, xs = x_prompt, x_sample
    lru_h_p, lru_h_s, lru_c_p, lru_c_s, swa_p, swa_s = [], [], [], [], [], []
    dil_p, dil_s = [[], [], []], [[], [], []]
    for l in range(DEPTH):
        j = l // 2
        if l % 2 == 0:
            ew = (w_in_even[j], conv_w[j], conv_b[j], w_rgate[j], b_rgate[j], w_igate[j], b_igate[j],
                  lru_lambda[j], swa_sinks[j], w_out_even[j])
            mp, (hp, cp, kvp) = even_mixer(xp, *ew, past=None)
            ms, (hs, cs, kvs) = even_mixer(xs, *ew, past=(state_lru_h[j], state_lru_conv[j], cache_swa_kv[j]))
            lru_h_p.append(hp)
            lru_h_s.append(hs)
            lru_c_p.append(cp)
            lru_c_s.append(cs)
            swa_p.append(kvp)
            swa_s.append(kvs)
        else:
            mp, newp = odd_mixer(xp, w_in_odd[j], w_out_odd[j], past=None)
            ms, news = odd_mixer(xs, w_in_odd[j], w_out_odd[j],
                                 past=(cache_dil1_kv[j], cache_dil2_kv[j], cache_dil3_kv[j]))
            for g in range(N_DIL):
                dil_p[g].append(newp[g])
                dil_s[g].append(news[g])
        xp = post_norm(xp, mp, ln1_g[l], ln1_b[l])
        xs = post_norm(xs, ms, ln1_g[l], ln1_b[l])
        mw = (w_router[l], router_bias[l], w_exp_gate[l], w_exp_up[l], w_exp_down[l],
              w_sh_gate[l], w_sh_up[l], w_sh_down[l])
        xp = post_norm(xp, moe_ffn(xp, *mw), ln2_g[l], ln2_b[l])
        xs = post_norm(xs, moe_ffn(xs, *mw), ln2_g[l], ln2_b[l])
    y_prompt, y_sample = xp, xs
    lru_h_prompt, lru_h_sample = jnp.stack(lru_h_p), jnp.stack(lru_h_s)
    lru_conv_prompt, lru_conv_sample = jnp.stack(lru_c_p), jnp.stack(lru_c_s)
    swa_kv_prompt, swa_kv_sample = jnp.stack(swa_p), jnp.stack(swa_s)
    dil1_kv_prompt, dil1_kv_sample = jnp.stack(dil_p[0]), jnp.stack(dil_s[0])
    dil2_kv_prompt, dil2_kv_sample = jnp.stack(dil_p[1]), jnp.stack(dil_s[1])
    dil3_kv_prompt, dil3_kv_sample = jnp.stack(dil_p[2]), jnp.stack(dil_s[2])
    return (y_prompt, y_sample, lru_h_prompt, lru_h_sample, lru_conv_prompt, lru_conv_sample,
            swa_kv_prompt, swa_kv_sample, dil1_kv_prompt, dil1_kv_sample, dil2_kv_prompt, dil2_kv_sample,
            dil3_kv_prompt, dil3_kv_sample)
```

```python
import functools

import numpy as np
import jax
import jax.numpy as jnp
from jax import lax
from jax.experimental import pallas as pl
from jax.experimental.pallas import tpu as pltpu

F32 = jnp.float32
BF16 = jnp.bfloat16

D_MODEL = 1024
HEAD_DIM = 64
LRU_WIDTH = 512
LRU_C = 8.0
CONV_W = 4
SWA_HEADS = 8
SWA_KV_HEADS = 2
SWA_GROUP = SWA_HEADS // SWA_KV_HEADS
SWA_WINDOW = 128
DIL_WINDOWS = (128, 512, 2048)
DIL_RATES = (1, 4, 16)
N_DIL = 3
DIL_HEADS = 8
BAND = 128
N_EXPERTS = 64
N_EXPERT_GROUPS = 8
GROUP_SIZE = N_EXPERTS // N_EXPERT_GROUPS
TOPK_GROUPS = 4
TOP_K = 8
D_EXPERT = 256
ROUTED_SCALE = 2.5
LN_EPS = 1e-5
DEPTH = 4
DEEPNORM_ALPHA = (2 * DEPTH) ** 0.25

LANES = 128
SUBLANES = 8
ROW_TILES = D_MODEL // LANES
VMEM_LIMIT = 48 * 1024 * 1024

NEG_INF = float("-inf")


def _cparams(n_grid_dims):
    return pltpu.CompilerParams(dimension_semantics=("arbitrary",) * n_grid_dims, vmem_limit_bytes=VMEM_LIMIT)


def _alibi_slopes(n):
    return np.exp2(-8.0 * np.arange(1, n + 1, dtype=np.float64) / n)


def _bdot(a, b):
    return jnp.dot(a.astype(BF16), b.astype(BF16), preferred_element_type=F32)


def _layer_norm(z, g, b):
    zc = z - jnp.mean(z, -1, keepdims=True)
    var = jnp.mean(zc * zc, -1, keepdims=True)
    return zc * lax.rsqrt(var + LN_EPS) * g + b


def _store_tile_rows(ref, val, rows):
    for c in range(ROW_TILES):
        ref[pl.ds(c, rows, stride=ROW_TILES), :] = val[:, c * LANES:(c + 1) * LANES]


def _load_tile_rows(ref, rows):
    return jnp.concatenate([ref[pl.ds(c, rows, stride=ROW_TILES), :] for c in range(ROW_TILES)], axis=-1)


def _mm_kernel(x_ref, w_ref, o_ref):
    o_ref[...] = _bdot(x_ref[...], w_ref[...])


def _matmul(x, w, tm, tn, name):
    m, k = x.shape
    n = w.shape[1]
    return pl.pallas_call(
        _mm_kernel,
        grid=(m // tm, n // tn),
        in_specs=[pl.BlockSpec((tm, k), lambda i, j: (i, 0)), pl.BlockSpec((k, tn), lambda i, j: (0, j))],
        out_specs=pl.BlockSpec((tm, tn), lambda i, j: (i, j)),
        out_shape=jax.ShapeDtypeStruct((m, n), F32),
        compiler_params=_cparams(2),
        name=name,
    )(x, w)


def _outproj_even_kernel(x_ref, a_ref, o_ref, w_ref, g_ref, b_ref, y_ref, yt_ref, wbf_ref):
    @pl.when(pl.program_id(0) == 0)
    def _():
        wbf_ref[...] = w_ref[...].astype(BF16)

    half = a_ref.shape[1]
    f = jnp.dot(a_ref[...].astype(BF16), wbf_ref[0:half, :], preferred_element_type=F32)
    f = f + jnp.dot(o_ref[...].astype(BF16), wbf_ref[half:, :], preferred_element_type=F32)
    y = _layer_norm(DEEPNORM_ALPHA * x_ref[...] + f, g_ref[...], b_ref[...])
    y_ref[...] = y
    _store_tile_rows(yt_ref, y, y.shape[0])


def _outproj_even(x, a, o, w, g, b, tm):
    m = x.shape[0]
    row = lambda i: (i, 0)
    fixed = lambda i: (0, 0)
    return pl.pallas_call(
        _outproj_even_kernel,
        grid=(m // tm,),
        in_specs=[pl.BlockSpec((tm, D_MODEL), row), pl.BlockSpec((tm, a.shape[1]), row),
                  pl.BlockSpec((tm, o.shape[1]), row), pl.BlockSpec(w.shape, fixed),
                  pl.BlockSpec((1, D_MODEL), fixed), pl.BlockSpec((1, D_MODEL), fixed)],
        out_specs=[pl.BlockSpec((tm, D_MODEL), row), pl.BlockSpec((tm * ROW_TILES, LANES), row)],
        out_shape=[jax.ShapeDtypeStruct((m, D_MODEL), F32), jax.ShapeDtypeStruct((m * ROW_TILES, LANES), F32)],
        scratch_shapes=[pltpu.VMEM(w.shape, BF16)],
        compiler_params=_cparams(1),
        name="outproj_even",
    )(x, a, o, w, g.reshape(1, -1), b.reshape(1, -1))


def _outproj_odd_kernel(x_ref, o1_ref, o2_ref, o3_ref, l1_ref, l2_ref, l3_ref, et_ref, w_ref, g_ref, b_ref,
                        y_ref, yt_ref, wbf_ref):
    @pl.when(pl.program_id(0) == 0)
    def _():
        wbf_ref[...] = w_ref[...].astype(BF16)

    l1, l2, l3 = l1_ref[...], l2_ref[...], l3_ref[...]
    m = jnp.maximum(jnp.maximum(l1, l2), l3)
    e1, e2, e3 = jnp.exp(l1 - m), jnp.exp(l2 - m), jnp.exp(l3 - m)
    inv = 1.0 / (e1 + e2 + e3)
    et = et_ref[...]
    expand = lambda wgt: jnp.dot(wgt, et, preferred_element_type=F32, precision=lax.Precision.HIGHEST)
    mix = expand(e1 * inv) * o1_ref[...] + expand(e2 * inv) * o2_ref[...] + expand(e3 * inv) * o3_ref[...]
    f = jnp.dot(mix.astype(BF16), wbf_ref[...], preferred_element_type=F32)
    y = _layer_norm(DEEPNORM_ALPHA * x_ref[...] + f, g_ref[...], b_ref[...])
    y_ref[...] = y
    _store_tile_rows(yt_ref, y, y.shape[0])


def _head_expand_matrix(n_heads):
    et = np.zeros((LANES, n_heads * HEAD_DIM), np.float32)
    for h in range(n_heads):
        et[h, h * HEAD_DIM:(h + 1) * HEAD_DIM] = 1.0
    return jnp.asarray(et)


def _outproj_odd(x, os_, ls_, w, g, b, tm):
    m = x.shape[0]
    row = lambda i: (i, 0)
    fixed = lambda i: (0, 0)
    width = os_[0].shape[1]
    et = _head_expand_matrix(DIL_HEADS)
    return pl.pallas_call(
        _outproj_odd_kernel,
        grid=(m // tm,),
        in_specs=[pl.BlockSpec((tm, D_MODEL), row)] + [pl.BlockSpec((tm, width), row)] * 3
                 + [pl.BlockSpec((tm, LANES), row)] * 3
                 + [pl.BlockSpec(et.shape, fixed), pl.BlockSpec(w.shape, fixed),
                    pl.BlockSpec((1, D_MODEL), fixed), pl.BlockSpec((1, D_MODEL), fixed)],
        out_specs=[pl.BlockSpec((tm, D_MODEL), row), pl.BlockSpec((tm * ROW_TILES, LANES), row)],
        out_shape=[jax.ShapeDtypeStruct((m, D_MODEL), F32), jax.ShapeDtypeStruct((m * ROW_TILES, LANES), F32)],
        scratch_shapes=[pltpu.VMEM(w.shape, BF16)],
        compiler_params=_cparams(1),
        name="outproj_odd",
    )(x, *os_, *ls_, et, w, g.reshape(1, -1), b.reshape(1, -1))


def _band_bias(slopes, dist_scale, window):
    qi = np.arange(BAND)[:, None]
    kj = np.arange(2 * BAND)[None, :]
    dist = qi + BAND - kj
    valid = (dist >= 0) & (dist <= window)
    bias = -slopes[:, None, None] * (dist_scale * dist)[None].astype(np.float64)
    return jnp.asarray(np.where(valid[None], bias, -np.inf).astype(np.float32))


def _band_kernel(*refs, n_heads, group, has_sink):
    if has_sink:
        q_ref, kp_ref, kc_ref, vp_ref, vc_ref, bias_ref, sink_ref, o_ref = refs
    else:
        q_ref, kp_ref, kc_ref, vp_ref, vc_ref, bias_ref, o_ref, lse_ref = refs
    j = pl.program_id(2)
    q = q_ref[0]
    k = jnp.concatenate([kp_ref[0], kc_ref[0]], axis=0)
    v = jnp.concatenate([vp_ref[0], vc_ref[0]], axis=0)
    col = lax.broadcasted_iota(jnp.int32, (BAND, 2 * BAND), 1)
    keep = (col >= BAND) | (j > 0)
    lane = lax.broadcasted_iota(jnp.int32, (BAND, LANES), 1)
    lse_all = jnp.zeros((BAND, LANES), F32)
    for h in range(n_heads):
        kh = h // group
        qh = q[:, h * HEAD_DIM:(h + 1) * HEAD_DIM].astype(BF16)
        kk = k[:, kh * HEAD_DIM:(kh + 1) * HEAD_DIM].astype(BF16)
        vv = v[:, kh * HEAD_DIM:(kh + 1) * HEAD_DIM].astype(BF16)
        s = lax.dot_general(qh, kk, (((1,), (1,)), ((), ())), preferred_element_type=F32)
        s = s * (HEAD_DIM ** -0.5) + bias_ref[h]
        s = jnp.where(keep, s, NEG_INF)
        m = jnp.max(s, axis=-1, keepdims=True)
        e = jnp.exp(s - m)
        l = jnp.sum(e, axis=-1, keepdims=True)
        o = jnp.dot(e.astype(BF16), vv, preferred_element_type=F32) / l
        lse = m + jnp.log(l)
        if has_sink:
            o = o * jax.nn.sigmoid(lse - sink_ref[h])
        else:
            lse_all = jnp.where(lane == h, lse, lse_all)
        o_ref[0, :, h * HEAD_DIM:(h + 1) * HEAD_DIM] = o
    if not has_sink:
        lse_ref[0] = lse_all


def _banded_attention(src, batch, lf, fold, qcol, kcol, vcol, kv_width, n_heads, group, slopes, dist_scale,
                      sinks, name):
    nb = lf // BAND
    qw = n_heads * HEAD_DIM
    bias = _band_bias(slopes, dist_scale, BAND)
    qmap = lambda b, r, j: (b, j, r * qcol[0] + qcol[1])
    kprev = lambda b, r, j: (b, jnp.maximum(j - 1, 0), r * kcol[0] + kcol[1])
    kcur = lambda b, r, j: (b, j, r * kcol[0] + kcol[1])
    vprev = lambda b, r, j: (b, jnp.maximum(j - 1, 0), r * vcol[0] + vcol[1])
    vcur = lambda b, r, j: (b, j, r * vcol[0] + vcol[1])
    in_specs = [pl.BlockSpec((1, BAND, qw), qmap),
                pl.BlockSpec((1, BAND, kv_width), kprev), pl.BlockSpec((1, BAND, kv_width), kcur),
                pl.BlockSpec((1, BAND, kv_width), vprev), pl.BlockSpec((1, BAND, kv_width), vcur),
                pl.BlockSpec(bias.shape, lambda b, r, j: (0, 0, 0))]
    args = [src, src, src, src, src, bias]
    omap = lambda b, r, j: (b, j, r)
    out_specs = [pl.BlockSpec((1, BAND, qw), omap)]
    out_shape = [jax.ShapeDtypeStruct((batch, lf, fold * qw), F32)]
    if sinks is not None:
        in_specs.append(pl.BlockSpec(memory_space=pltpu.SMEM))
        args.append(sinks)
    else:
        out_specs.append(pl.BlockSpec((1, BAND, LANES), omap))
        out_shape.append(jax.ShapeDtypeStruct((batch, lf, fold * LANES), F32))
    return pl.pallas_call(
        functools.partial(_band_kernel, n_heads=n_heads, group=group, has_sink=sinks is not None),
        grid=(batch, fold, nb),
        in_specs=in_specs, out_specs=out_specs, out_shape=out_shape,
        compiler_params=_cparams(3),
        name=name,
    )(*args)


def _gelu_tanh(x):
    return 0.5 * x * (1.0 + jnp.tanh(0.7978845608028654 * (x + 0.044715 * x * x * x)))


def _softplus(z):
    return jnp.maximum(z, 0.0) + jnp.log1p(jnp.exp(-jnp.abs(z)))


def _lru_gates(xc, wr, br, wi, bi, lam):
    r = jax.nn.sigmoid(_bdot(xc, wr) + br)
    i = jax.nn.sigmoid(_bdot(xc, wi) + bi)
    log_a = -LRU_C * r * _softplus(-lam)
    a = jnp.exp(log_a)
    th = jnp.tanh(log_a)
    u = jnp.sqrt(-2.0 * th / (1.0 - th)) * (i * xc)
    return a, u


def _lru_prompt_kernel(xr_ref, yg_ref, cw_ref, cb_ref, wr_ref, br_ref, wi_ref, bi_ref, lam_ref,
                       aout_ref, hlast_ref, xpad_ref, a_ref, u_ref):
    length = xr_ref.shape[1]
    xr = xr_ref[0]
    xpad_ref[0:SUBLANES, :] = jnp.zeros((SUBLANES, LANES), F32)
    xpad_ref[SUBLANES:, :] = xr
    cw = cw_ref[...]
    xc = cb_ref[...] + cw[CONV_W - 1:CONV_W, :] * xr
    for j in range(CONV_W - 1):
        shift = CONV_W - 1 - j
        xc = xc + cw[j:j + 1, :] * xpad_ref[pl.ds(SUBLANES - shift, length), :]
    a, u = _lru_gates(xc, wr_ref[0], br_ref[...], wi_ref[0], bi_ref[...], lam_ref[...])
    a_ref[...] = a
    u_ref[...] = u
    row = lax.broadcasted_iota(jnp.int32, (SUBLANES, LANES), 0)

    def body(blk, carry):
        off = pl.multiple_of(blk * SUBLANES, SUBLANES)
        ab = a_ref[pl.ds(off, SUBLANES), :]
        ub = u_ref[pl.ds(off, SUBLANES), :]
        for d in (1, 2, 4):
            ush = jnp.where(row >= d, pltpu.roll(ub, d, 0), 0.0)
            ash = jnp.where(row >= d, pltpu.roll(ab, d, 0), 1.0)
            ub = ab * ush + ub
            ab = ab * ash
        h = ab * carry + ub
        u_ref[pl.ds(off, SUBLANES), :] = h
        return jnp.broadcast_to(h[SUBLANES - 1:SUBLANES, :], (SUBLANES, LANES))

    carry = lax.fori_loop(0, length // SUBLANES, body, jnp.zeros((SUBLANES, LANES), F32), unroll=4)
    hlast_ref[0] = carry[0:1, :]
    aout_ref[0] = u_ref[...] * _gelu_tanh(yg_ref[0])


def _gate_blockdiag(w):
    z = jnp.zeros((4, LANES, LANES), F32)
    z = z.at[:, 0:HEAD_DIM, 0:HEAD_DIM].set(w[0::2])
    z = z.at[:, HEAD_DIM:, HEAD_DIM:].set(w[1::2])
    return z


def _lru_prompt(proj3, conv_w, conv_b, wr_bd, b_r, wi_bd, b_i, lam):
    batch, length, _ = proj3.shape
    n_ct = LRU_WIDTH // LANES
    vec = lambda b, c: (0, c)
    return pl.pallas_call(
        _lru_prompt_kernel,
        grid=(batch, n_ct),
        in_specs=[pl.BlockSpec((1, length, LANES), lambda b, c: (b, 0, c)),
                  pl.BlockSpec((1, length, LANES), lambda b, c: (b, 0, n_ct + c)),
                  pl.BlockSpec((CONV_W, LANES), vec), pl.BlockSpec((1, LANES), vec),
                  pl.BlockSpec((1, LANES, LANES), lambda b, c: (c, 0, 0)), pl.BlockSpec((1, LANES), vec),
                  pl.BlockSpec((1, LANES, LANES), lambda b, c: (c, 0, 0)), pl.BlockSpec((1, LANES), vec),
                  pl.BlockSpec((1, LANES), vec)],
        out_specs=[pl.BlockSpec((1, length, LANES), lambda b, c: (b, 0, c)),
                   pl.BlockSpec((1, 1, LANES), lambda b, c: (b, 0, c))],
        out_shape=[jax.ShapeDtypeStruct((batch, length, LRU_WIDTH), F32),
                   jax.ShapeDtypeStruct((batch, 1, LRU_WIDTH), F32)],
        scratch_shapes=[pltpu.VMEM((length + SUBLANES, LANES), F32), pltpu.VMEM((length, LANES), F32),
                        pltpu.VMEM((length, LANES), F32)],
        compiler_params=_cparams(2),
        name="lru_prompt",
    )(proj3, proj3, conv_w, conv_b.reshape(1, -1), wr_bd, b_r.reshape(1, -1), wi_bd, b_i.reshape(1, -1),
      lam.reshape(1, -1))


def _lru_decode_kernel(xr_ref, yg_ref, cbuf_ref, h0_ref, cw_ref, cb_ref, wr_ref, br_ref, wi_ref, bi_ref, lam_ref,
                       aout_ref, hlast_ref):
    steps = xr_ref.shape[0]
    xs = [cbuf_ref[j] for j in range(CONV_W - 1)] + [xr_ref[t] for t in range(steps)]
    cw = cw_ref[...]
    h = h0_ref[...]
    n_ct = LRU_WIDTH // LANES
    for t in range(steps):
        xc = cb_ref[...]
        for j in range(CONV_W):
            xc = xc + cw[j:j + 1, :] * xs[t + j]
        parts = []
        for c in range(n_ct):
            sl = slice(c * LANES, (c + 1) * LANES)
            parts.append(_lru_gates(xc[:, sl], wr_ref[c], br_ref[:, sl], wi_ref[c], bi_ref[:, sl], lam_ref[:, sl]))
        a = jnp.concatenate([p[0] for p in parts], axis=-1)
        u = jnp.concatenate([p[1] for p in parts], axis=-1)
        h = a * h + u
        aout_ref[t] = h * _gelu_tanh(yg_ref[t])
    hlast_ref[...] = h


def _lru_decode(proj3, cbuf, h0, conv_w, conv_b, wr_bd, b_r, wi_bd, b_i, lam):
    steps, batch, _ = proj3.shape
    full2 = lambda i: (0, 0)
    full3 = lambda i: (0, 0, 0)
    return pl.pallas_call(
        _lru_decode_kernel,
        grid=(1,),
        in_specs=[pl.BlockSpec((steps, batch, LRU_WIDTH), lambda i: (0, 0, 0)),
                  pl.BlockSpec((steps, batch, LRU_WIDTH), lambda i: (0, 0, 1)),
                  pl.BlockSpec(cbuf.shape, full3), pl.BlockSpec(h0.shape, full2),
                  pl.BlockSpec(conv_w.shape, full2), pl.BlockSpec((1, LRU_WIDTH), full2),
                  pl.BlockSpec(wr_bd.shape, full3), pl.BlockSpec((1, LRU_WIDTH), full2),
                  pl.BlockSpec(wi_bd.shape, full3), pl.BlockSpec((1, LRU_WIDTH), full2),
                  pl.BlockSpec((1, LRU_WIDTH), full2)],
        out_specs=[pl.BlockSpec((steps, batch, LRU_WIDTH), full3), pl.BlockSpec((batch, LRU_WIDTH), full2)],
        out_shape=[jax.ShapeDtypeStruct((steps, batch, LRU_WIDTH), F32),
                   jax.ShapeDtypeStruct((batch, LRU_WIDTH), F32)],
        compiler_params=_cparams(1),
        name="lru_decode",
    )(proj3, proj3, cbuf, h0, conv_w, conv_b.reshape(1, -1), wr_bd, b_r.reshape(1, -1), wi_bd,
      b_i.reshape(1, -1), lam.reshape(1, -1))


def _decode_bias(slopes_hq, q_times, window, dil, steps):
    heads, nq = slopes_hq.shape
    bias = np.zeros((heads, nq, window + LANES), np.float64)
    w = np.arange(window)
    for qi, t in enumerate(q_times):
        if t < 0:
            continue
        dist = window + t - w
        ok = (dist % dil == 0) & (dist <= window)
        bias[:, qi, :window] = np.where(ok[None], -slopes_hq[:, qi, None] * dist[None], -np.inf)
        for c in range(LANES):
            d = t - c
            good = c < steps and d >= 0 and d % dil == 0
            bias[:, qi, window + c] = -slopes_hq[:, qi] * d if good else -np.inf
    return jnp.asarray(bias.astype(np.float32))


def _decode_kernel(*refs, window, steps, has_sink, has_alias):
    refs = list(refs)
    q_ref, c_ref, new_ref, tail_ref, bias_ref = refs[:5]
    pos = 5
    sink_ref = None
    if has_sink:
        sink_ref = refs[pos]
        pos += 1
    if has_alias:
        pos += 1
    co_ref, o_ref, lse_ref = refs[pos:pos + 3]
    c = c_ref[0, 0]
    new = new_ref[0]
    k_ext = jnp.concatenate([c[0], new[0]], axis=-1).astype(BF16)
    v_ext = jnp.concatenate([c[1], new[1]], axis=-1).astype(BF16)
    q = q_ref[0].astype(BF16)
    s = jnp.einsum("hqd,hdw->hqw", q, k_ext, preferred_element_type=F32) * (HEAD_DIM ** -0.5) + bias_ref[...]
    m = jnp.max(s, axis=-1, keepdims=True)
    e = jnp.exp(s - m)
    l = jnp.sum(e, axis=-1, keepdims=True)
    lse = m + jnp.log(l)
    p = e / l
    if has_sink:
        p = p * jax.nn.sigmoid(lse - sink_ref[:, :, 0:1])
    o_ref[0] = jnp.einsum("hqw,hdw->hqd", p.astype(BF16), v_ext, preferred_element_type=F32)
    lse_ref[0] = jnp.broadcast_to(lse, lse_ref.shape[1:])
    co_ref[0, 0] = pltpu.roll(c, window - steps, 3)
    lane = lax.broadcasted_iota(jnp.int32, tail_ref.shape[1:], 3)
    last = pltpu.roll(c[:, :, :, window - LANES:], LANES - steps, 3)
    co_ref[0, 0, :, :, :, window - LANES:] = jnp.where(lane >= LANES - steps, tail_ref[0], last)


def _decode_attention(layer, cache_t, prev_out, q, new_front, new_tail, bias, sink, heads_per_step, steps, name):
    n_layers, batch, _, heads, hd, window = cache_t.shape
    nq = q.shape[2]
    hb = heads_per_step
    cmap = lambda b, h: (layer, b, 0, h, 0, 0)
    in_specs = [pl.BlockSpec((1, hb, nq, hd), lambda b, h: (b, h, 0, 0)),
                pl.BlockSpec((1, 1, 2, hb, hd, window), cmap),
                pl.BlockSpec((1, 2, hb, hd, LANES), lambda b, h: (b, 0, h, 0, 0)),
                pl.BlockSpec((1, 2, hb, hd, LANES), lambda b, h: (b, 0, h, 0, 0)),
                pl.BlockSpec((hb, nq, window + LANES), lambda b, h: (h, 0, 0))]
    args = [q, cache_t, new_front, new_tail, bias]
    if sink is not None:
        in_specs.append(pl.BlockSpec((hb, nq, LANES), lambda b, h: (h, 0, 0)))
        args.append(sink)
    aliases = {}
    if prev_out is not None:
        in_specs.append(pl.BlockSpec(memory_space=pl.ANY))
        aliases = {len(args): 0}
        args.append(prev_out)
    return pl.pallas_call(
        functools.partial(_decode_kernel, window=window, steps=steps, has_sink=sink is not None,
                          has_alias=prev_out is not None),
        grid=(batch, heads // hb),
        in_specs=in_specs,
        out_specs=[pl.BlockSpec((1, 1, 2, hb, hd, window), cmap),
                   pl.BlockSpec((1, hb, nq, hd), lambda b, h: (b, h, 0, 0)),
                   pl.BlockSpec((1, hb, nq, LANES), lambda b, h: (b, h, 0, 0))],
        out_shape=[jax.ShapeDtypeStruct(cache_t.shape, F32),
                   jax.ShapeDtypeStruct((batch, heads, nq, hd), F32),
                   jax.ShapeDtypeStruct((batch, heads, nq, LANES), F32)],
        input_output_aliases=aliases,
        compiler_params=_cparams(2),
        name=name,
    )(*args)


def _router_kernel(x_ref, wt_ref, b_ref, e_ref, g_ref):
    logits = lax.dot_general(wt_ref[...].astype(BF16), x_ref[...].astype(BF16), (((1,), (1,)), ((), ())),
                             preferred_element_type=F32)
    scores = jax.nn.sigmoid(logits)
    biased = scores + b_ref[...]
    tm = biased.shape[1]
    b3 = biased.reshape(N_EXPERT_GROUPS, GROUP_SIZE, tm)
    i3 = lax.broadcasted_iota(jnp.int32, b3.shape, 1).astype(F32)
    g1 = jnp.max(b3, axis=1, keepdims=True)
    first = jnp.min(jnp.where(b3 == g1, i3, float(GROUP_SIZE)), axis=1, keepdims=True)
    g2 = jnp.max(jnp.where(i3 == first, NEG_INF, b3), axis=1, keepdims=True)
    gs = g1 + g2
    gi = lax.broadcasted_iota(jnp.int32, gs.shape, 0).astype(F32)
    sel = jnp.zeros(gs.shape, F32)
    for _ in range(TOPK_GROUPS):
        m = jnp.max(gs, axis=0, keepdims=True)
        f = jnp.min(jnp.where(gs == m, gi, float(N_EXPERT_GROUPS)), axis=0, keepdims=True)
        hit = gi == f
        sel = jnp.where(hit, 1.0, sel)
        gs = jnp.where(hit, NEG_INF, gs)
    masked = jnp.where(sel > 0.5, b3, NEG_INF).reshape(N_EXPERTS, tm)
    ei = lax.broadcasted_iota(jnp.int32, masked.shape, 0).astype(F32)
    e_rows, g_rows = [], []
    for _ in range(TOP_K):
        m = jnp.max(masked, axis=0, keepdims=True)
        f = jnp.min(jnp.where(masked == m, ei, float(N_EXPERTS)), axis=0, keepdims=True)
        hit = ei == f
        g_rows.append(jnp.sum(jnp.where(hit, scores, 0.0), axis=0, keepdims=True))
        e_rows.append(f)
        masked = jnp.where(hit, NEG_INF, masked)
    gates = jnp.concatenate(g_rows, axis=0)
    gates = gates / jnp.sum(gates, axis=0, keepdims=True) * ROUTED_SCALE
    e_ref[...] = jnp.concatenate(e_rows, axis=0).astype(jnp.int32)
    g_ref[...] = gates


def _router(x, w_router_t, bias, tm):
    n = x.shape[0]
    return pl.pallas_call(
        _router_kernel,
        grid=(n // tm,),
        in_specs=[pl.BlockSpec((tm, D_MODEL), lambda i: (i, 0)), pl.BlockSpec((N_EXPERTS, D_MODEL), lambda i: (0, 0)),
                  pl.BlockSpec((N_EXPERTS, 1), lambda i: (0, 0))],
        out_specs=[pl.BlockSpec((TOP_K, tm), lambda i: (0, i)), pl.BlockSpec((TOP_K, tm), lambda i: (0, i))],
        out_shape=[jax.ShapeDtypeStruct((TOP_K, n), jnp.int32), jax.ShapeDtypeStruct((TOP_K, n), F32)],
        compiler_params=_cparams(1),
        name="moe_router",
    )(x, w_router_t, bias.reshape(N_EXPERTS, 1))


def _dispatch_plan(top_e, gates, tm):
    n = top_e.shape[1]
    n_pairs = TOP_K * n
    flat_e = top_e.reshape(-1)
    order = jnp.argsort(flat_e, stable=True).astype(jnp.int32)
    e_sorted = flat_e[order]
    counts = jnp.sum(flat_e[:, None] == jnp.arange(N_EXPERTS, dtype=jnp.int32)[None, :], axis=0, dtype=jnp.int32)
    padded = (counts + tm - 1) // tm * tm
    pend = jnp.cumsum(padded)
    pstart = pend - padded
    start = jnp.cumsum(counts) - counts
    dest = pstart[e_sorted] + (jnp.arange(n_pairs, dtype=jnp.int32) - start[e_sorted])
    nb = -(-n_pairs // tm) + N_EXPERTS
    n_rows = nb * tm
    row_pair = jnp.full((n_rows,), -1, jnp.int32).at[dest].set(order)
    row_gate = jnp.zeros((n_rows,), F32).at[dest].set(gates.reshape(-1)[order])
    r = jnp.arange(n_rows, dtype=jnp.int32)
    dump = n_pairs + ((r // tm) % 2) * tm + r % tm
    row_tok = jnp.where(row_pair >= 0, row_pair % n, 0)
    row_dst = jnp.where(row_pair >= 0, row_pair, dump)
    block_e = jnp.minimum(jnp.searchsorted(pend, jnp.arange(nb, dtype=jnp.int32) * tm, side="right"),
                          N_EXPERTS - 1).astype(jnp.int32)
    return row_tok.reshape(nb, 1, tm), row_dst.reshape(nb, 1, tm), row_gate.reshape(n_rows, 1), block_e


def _expert_kernel(be_ref, tok_ref, tokn_ref, dst_ref, gate_ref, x_hbm, wg_ref, wu_ref, wd_ref, y_hbm,
                   xbuf, obuf, gsem, ssem, *, tm):
    i = pl.program_id(0)
    nb = pl.num_programs(0)
    slot = i % 2

    def gather_copy(tok, r, s):
        return pltpu.make_async_copy(x_hbm.at[pl.ds(tok * ROW_TILES, ROW_TILES)],
                                     xbuf.at[s, pl.ds(r * ROW_TILES, ROW_TILES)], gsem.at[s])

    def scatter_copy(dst, r, s):
        return pltpu.make_async_copy(obuf.at[s, pl.ds(r * ROW_TILES, ROW_TILES)],
                                     y_hbm.at[pl.ds(dst * ROW_TILES, ROW_TILES)], ssem.at[s])

    def start_gathers(ref, s):
        def body(r, carry):
            gather_copy(ref[0, 0, r], r, s).start()
            return carry
        lax.fori_loop(0, tm, body, 0, unroll=8)

    def wait_gathers(s):
        def body(r, carry):
            gather_copy(0, r, s).wait()
            return carry
        lax.fori_loop(0, tm, body, 0, unroll=8)

    def wait_scatters(s):
        def body(r, carry):
            scatter_copy(0, r, s).wait()
            return carry
        lax.fori_loop(0, tm, body, 0, unroll=8)

    @pl.when(i == 0)
    def _():
        start_gathers(tok_ref, 0)

    @pl.when(i + 1 < nb)
    def _():
        start_gathers(tokn_ref, 1 - slot)

    wait_gathers(slot)

    @pl.when(i >= 2)
    def _():
        wait_scatters(slot)

    x = _load_tile_rows(xbuf.at[slot], tm).astype(BF16)
    hg = jnp.dot(x, wg_ref[0].astype(BF16), preferred_element_type=F32)
    hu = jnp.dot(x, wu_ref[0].astype(BF16), preferred_element_type=F32)
    hid = (hg * jax.nn.sigmoid(hg)) * hu
    out = jnp.dot(hid.astype(BF16), wd_ref[0].astype(BF16), preferred_element_type=F32) * gate_ref[...]
    _store_tile_rows(obuf.at[slot], out, tm)

    def body(r, carry):
        scatter_copy(dst_ref[0, 0, r], r, slot).start()
        return carry
    lax.fori_loop(0, tm, body, 0, unroll=8)

    @pl.when(i == nb - 1)
    def _():
        wait_scatters(slot)
        wait_scatters(1 - slot)


def _experts(x_tiles, row_tok, row_dst, row_gate, block_e, w_gate, w_up, w_down, n_tokens, tm):
    nb = row_tok.shape[0]
    n_rows_out = TOP_K * n_tokens + 2 * tm
    smem_blk = lambda f: pl.BlockSpec((1, 1, tm), f, memory_space=pltpu.SMEM)
    wmap = lambda i, be: (be[i], 0, 0)
    grid_spec = pltpu.PrefetchScalarGridSpec(
        num_scalar_prefetch=1,
        grid=(nb,),
        in_specs=[smem_blk(lambda i, be: (i, 0, 0)),
                  smem_blk(lambda i, be: (jnp.minimum(i + 1, nb - 1), 0, 0)),
                  smem_blk(lambda i, be: (i, 0, 0)),
                  pl.BlockSpec((tm, 1), lambda i, be: (i, 0)),
                  pl.BlockSpec(memory_space=pl.ANY),
                  pl.BlockSpec((1, D_MODEL, D_EXPERT), wmap), pl.BlockSpec((1, D_MODEL, D_EXPERT), wmap),
                  pl.BlockSpec((1, D_EXPERT, D_MODEL), wmap)],
        out_specs=pl.BlockSpec(memory_space=pl.ANY),
        scratch_shapes=[pltpu.VMEM((2, tm * ROW_TILES, LANES), F32), pltpu.VMEM((2, tm * ROW_TILES, LANES), F32),
                        pltpu.SemaphoreType.DMA((2,)), pltpu.SemaphoreType.DMA((2,))],
    )
    return pl.pallas_call(
        functools.partial(_expert_kernel, tm=tm),
        grid_spec=grid_spec,
        out_shape=jax.ShapeDtypeStruct((n_rows_out * ROW_TILES, LANES), F32),
        compiler_params=_cparams(1),
        name="moe_experts",
    )(block_e, row_tok, row_tok, row_dst, row_gate, x_tiles, w_gate, w_up, w_down)


def _combine_kernel(*refs, tm):
    y_refs = refs[:TOP_K]
    x_ref, wsg_ref, wsu_ref, wsd_ref, g_ref, b_ref, o_ref, wsg_bf, wsu_bf, wsd_bf = refs[TOP_K:]

    @pl.when(pl.program_id(0) == 0)
    def _():
        wsg_bf[...] = wsg_ref[...].astype(BF16)
        wsu_bf[...] = wsu_ref[...].astype(BF16)
        wsd_bf[...] = wsd_ref[...].astype(BF16)

    routed = _load_tile_rows(y_refs[0], tm)
    for k in range(1, TOP_K):
        routed = routed + _load_tile_rows(y_refs[k], tm)
    x = x_ref[...]
    xb = x.astype(BF16)
    sg = jnp.dot(xb, wsg_bf[...], preferred_element_type=F32)
    su = jnp.dot(xb, wsu_bf[...], preferred_element_type=F32)
    shared = jnp.dot(((sg * jax.nn.sigmoid(sg)) * su).astype(BF16), wsd_bf[...], preferred_element_type=F32)
    o_ref[...] = _layer_norm(DEEPNORM_ALPHA * x + (routed + shared), g_ref[...], b_ref[...])


def _combine(y_pairs, x, ws_gate, ws_up, ws_down, g, b, tm):
    n = x.shape[0]
    nt = n // tm
    fixed = lambda i: (0, 0)
    y_specs = [pl.BlockSpec((tm * ROW_TILES, LANES), (lambda i, k=k: (k * nt + i, 0))) for k in range(TOP_K)]
    return pl.pallas_call(
        functools.partial(_combine_kernel, tm=tm),
        grid=(nt,),
        in_specs=y_specs + [pl.BlockSpec((tm, D_MODEL), lambda i: (i, 0)),
                            pl.BlockSpec(ws_gate.shape, fixed), pl.BlockSpec(ws_up.shape, fixed),
                            pl.BlockSpec(ws_down.shape, fixed),
                            pl.BlockSpec((1, D_MODEL), fixed), pl.BlockSpec((1, D_MODEL), fixed)],
        out_specs=pl.BlockSpec((tm, D_MODEL), lambda i: (i, 0)),
        out_shape=jax.ShapeDtypeStruct((n, D_MODEL), F32),
        scratch_shapes=[pltpu.VMEM(ws_gate.shape, BF16), pltpu.VMEM(ws_up.shape, BF16),
                        pltpu.VMEM(ws_down.shape, BF16)],
        compiler_params=_cparams(1),
        name="moe_combine",
    )(*([y_pairs] * TOP_K), x, ws_gate, ws_up, ws_down, g.reshape(1, -1), b.reshape(1, -1))


def _moe_block(x, x_tiles, w_router_t, router_bias, w_gate, w_up, w_down, ws_gate, ws_up, ws_down, g, b,
               tm_route, tm_expert, tm_combine):
    n = x.shape[0]
    top_e, gates = _router(x, w_router_t, router_bias, tm_route)
    row_tok, row_dst, row_gate, block_e = _dispatch_plan(top_e, gates, tm_expert)
    y_pairs = _experts(x_tiles, row_tok, row_dst, row_gate, block_e, w_gate, w_up, w_down, n, tm_expert)
    return _combine(y_pairs, x, ws_gate, ws_up, ws_down, g, b, tm_combine)


def _to_cache_t(cache):
    return jnp.transpose(cache, (0, 1, 3, 4, 5, 2))


def _from_cache_t(cache_t):
    return jnp.transpose(cache_t, (0, 1, 5, 2, 3, 4))


def _new_kv_columns(k, v, steps, batch, heads):
    kv = jnp.stack([k, v]).reshape(2, steps, batch, heads, HEAD_DIM)
    kv = jnp.transpose(kv, (2, 0, 3, 4, 1))
    front = jnp.pad(kv, ((0, 0),) * 4 + ((0, LANES - steps),))
    tail = jnp.pad(kv, ((0, 0),) * 4 + ((LANES - steps, 0),))
    return front, tail


def kernel(x_prompt, x_sample, state_lru_h, state_lru_conv, cache_swa_kv, cache_dil1_kv, cache_dil2_kv,
           cache_dil3_kv, w_in_even, conv_w, conv_b, w_rgate, b_rgate, w_igate, b_igate, lru_lambda, swa_sinks,
           w_out_even, w_in_odd, w_out_odd, ln1_g, ln1_b, ln2_g, ln2_b, w_router, router_bias, w_exp_gate,
           w_exp_up, w_exp_down, w_sh_gate, w_sh_up, w_sh_down):
    batch, seq, _ = x_prompt.shape
    dbatch, steps, _ = x_sample.shape
    n_p = batch * seq
    n_s = dbatch * steps
    xp = x_prompt.reshape(n_p, D_MODEL)
    xs = jnp.transpose(x_sample, (1, 0, 2)).reshape(n_s, D_MODEL)

    slopes8 = _alibi_slopes(SWA_HEADS)
    pad_q = SUBLANES - steps
    q_times_dil = list(range(steps)) + [-1] * pad_q
    swa_q_times = [t for t in range(steps) for _ in range(SWA_GROUP)]
    swa_slopes = np.stack([np.tile(slopes8[kv * SWA_GROUP:(kv + 1) * SWA_GROUP], steps)
                           for kv in range(SWA_KV_HEADS)])
    swa_bias = _decode_bias(swa_slopes, swa_q_times, SWA_WINDOW, 1, steps)
    dil_bias = [_decode_bias(np.tile(_alibi_slopes(DIL_HEADS)[:, None], (1, SUBLANES)), q_times_dil,
                             DIL_WINDOWS[g], DIL_RATES[g], steps) for g in range(N_DIL)]

    swa_t = _to_cache_t(cache_swa_kv)
    dil_t = [_to_cache_t(c) for c in (cache_dil1_kv, cache_dil2_kv, cache_dil3_kv)]
    swa_out = None
    dil_out = [None] * N_DIL
    lru_h_p, lru_h_s, lru_c_p, lru_c_s, swa_p = [], [], [], [], []
    dil_p = [[] for _ in range(N_DIL)]

    for l in range(DEPTH):
        j = l // 2
        if l % 2 == 0:
            wr_bd, wi_bd = _gate_blockdiag(w_rgate[j]), _gate_blockdiag(w_igate[j])
            proj = _matmul(xp, w_in_even[j], 1024, 896, "inproj_even_prompt")
            proj3 = proj.reshape(batch, seq, -1)
            a_out, h_last = _lru_prompt(proj3, conv_w[j], conv_b[j], wr_bd, b_rgate[j], wi_bd, b_igate[j],
                                        lru_lambda[j])
            (o_swa,) = _banded_attention(proj3, batch, seq, 1, (0, 2), (0, 12), (0, 13), SWA_KV_HEADS * HEAD_DIM,
                                         SWA_HEADS, SWA_GROUP, slopes8, 1, swa_sinks[j], "swa_prompt")
            lru_h_p.append(h_last[:, 0])
            lru_c_p.append(proj3[:, seq - (CONV_W - 1):, :LRU_WIDTH])
            keep = min(SWA_WINDOW, seq)
            swa_p.append(proj3[:, seq - keep:, 2 * LRU_WIDTH + SWA_HEADS * HEAD_DIM:]
                         .reshape(batch, keep, 2, SWA_KV_HEADS, HEAD_DIM))
            xp, xp_tiles = _outproj_even(xp, a_out.reshape(n_p, -1), o_swa.reshape(n_p, -1), w_out_even[j],
                                         ln1_g[l], ln1_b[l], 512)
            proj_s = _matmul(xs, w_in_even[j], n_s, 896, "inproj_even_sample")
            proj_s3 = proj_s.reshape(steps, dbatch, -1)
            cbuf = jnp.transpose(state_lru_conv[j], (1, 0, 2))
            a_s, h_s = _lru_decode(proj_s3, cbuf, state_lru_h[j], conv_w[j], conv_b[j], wr_bd, b_rgate[j], wi_bd,
                                   b_igate[j], lru_lambda[j])
            lru_h_s.append(h_s)
            xr_s = proj_s3[:, :, :LRU_WIDTH]
            lru_c_s.append(jnp.transpose(jnp.concatenate([cbuf, xr_s], axis=0)[-(CONV_W - 1):], (1, 0, 2)))
            o0 = 2 * LRU_WIDTH
            q_s = proj_s[:, o0:o0 + SWA_HEADS * HEAD_DIM].reshape(steps, dbatch, SWA_KV_HEADS, SWA_GROUP, HEAD_DIM)
            q_s = jnp.transpose(q_s, (1, 2, 0, 3, 4)).reshape(dbatch, SWA_KV_HEADS, steps * SWA_GROUP, HEAD_DIM)
            o1 = o0 + SWA_HEADS * HEAD_DIM
            o2 = o1 + SWA_KV_HEADS * HEAD_DIM
            front, tail = _new_kv_columns(proj_s[:, o1:o2], proj_s[:, o2:], steps, dbatch, SWA_KV_HEADS)
            sink = jnp.broadcast_to(
                jnp.tile(swa_sinks[j].reshape(SWA_KV_HEADS, SWA_GROUP), (1, steps))[:, :, None],
                (SWA_KV_HEADS, steps * SWA_GROUP, LANES))
            swa_out, o_dec, _ = _decode_attention(j, swa_t, swa_out, q_s, front, tail, swa_bias, sink,
                                                  SWA_KV_HEADS, steps, "swa_decode")
            o_dec = o_dec.reshape(dbatch, SWA_KV_HEADS, steps, SWA_GROUP, HEAD_DIM)
            o_dec = jnp.transpose(o_dec, (2, 0, 1, 3, 4)).reshape(n_s, SWA_HEADS * HEAD_DIM)
            xs, xs_tiles = _outproj_even(xs, a_s.reshape(n_s, -1), o_dec, w_out_even[j], ln1_g[l], ln1_b[l], n_s)
        else:
            proj = _matmul(xp, w_in_odd[j], 1024, 768, "inproj_odd_prompt")
            row_w = proj.shape[1]
            os_, ls_ = [], []
            for g in range(N_DIL):
                dil = DIL_RATES[g]
                src = proj.reshape(batch, seq // dil, dil * row_w)
                n_blk = row_w // (DIL_HEADS * HEAD_DIM)
                o_g, lse_g = _banded_attention(src, batch, seq // dil, dil, (n_blk, 3 * g), (n_blk, 3 * g + 1),
                                               (n_blk, 3 * g + 2), DIL_HEADS * HEAD_DIM, DIL_HEADS, 1,
                                               _alibi_slopes(DIL_HEADS), dil, None, "dil_prompt")
                os_.append(o_g.reshape(n_p, -1))
                ls_.append(lse_g.reshape(n_p, LANES))
                keep = min(DIL_WINDOWS[g], seq)
                c0 = g * 3 * DIL_HEADS * HEAD_DIM + DIL_HEADS * HEAD_DIM
                dil_p[g].append(proj.reshape(batch, seq, row_w)[:, seq - keep:, c0:c0 + 2 * DIL_HEADS * HEAD_DIM]
                                .reshape(batch, keep, 2, DIL_HEADS, HEAD_DIM))
            xp, xp_tiles = _outproj_odd(xp, os_, ls_, w_out_odd[j], ln1_g[l], ln1_b[l], 512)
            proj_s = _matmul(xs, w_in_odd[j], n_s, 768, "inproj_odd_sample")
            os_, ls_ = [], []
            gw = DIL_HEADS * HEAD_DIM
            for g in range(N_DIL):
                c0 = g * 3 * gw
                q_s = proj_s[:, c0:c0 + gw].reshape(steps, dbatch, DIL_HEADS, HEAD_DIM)
                q_s = jnp.pad(jnp.transpose(q_s, (1, 2, 0, 3)), ((0, 0), (0, 0), (0, pad_q), (0, 0)))
                front, tail = _new_kv_columns(proj_s[:, c0 + gw:c0 + 2 * gw], proj_s[:, c0 + 2 * gw:c0 + 3 * gw],
                                              steps, dbatch, DIL_HEADS)
                dil_out[g], o_dec, lse_dec = _decode_attention(j, dil_t[g], dil_out[g], q_s, front, tail,
                                                               dil_bias[g], None, 4, steps, "dil_decode")
                o_dec = jnp.transpose(o_dec[:, :, :steps], (2, 0, 1, 3)).reshape(n_s, gw)
                lse_dec = jnp.transpose(lse_dec[:, :, :steps, 0], (2, 0, 1)).reshape(n_s, DIL_HEADS)
                os_.append(o_dec)
                ls_.append(jnp.pad(lse_dec, ((0, 0), (0, LANES - DIL_HEADS))))
            xs, xs_tiles = _outproj_odd(xs, os_, ls_, w_out_odd[j], ln1_g[l], ln1_b[l], n_s)

        moe_w = (jnp.transpose(w_router[l]), router_bias[l], w_exp_gate[l], w_exp_up[l], w_exp_down[l],
                 w_sh_gate[l], w_sh_up[l], w_sh_down[l], ln2_g[l], ln2_b[l])
        xp = _moe_block(xp, xp_tiles, *moe_w, tm_route=512, tm_expert=256, tm_combine=512)
        xs = _moe_block(xs, xs_tiles, *moe_w, tm_route=n_s, tm_expert=128, tm_combine=n_s)

    y_prompt = xp.reshape(batch, seq, D_MODEL)
    y_sample = jnp.transpose(xs.reshape(steps, dbatch, D_MODEL), (1, 0, 2))
    return (y_prompt, y_sample, jnp.stack(lru_h_p), jnp.stack(lru_h_s), jnp.stack(lru_c_p), jnp.stack(lru_c_s),
            jnp.stack(swa_p), _from_cache_t(swa_out),
            jnp.stack(dil_p[0]), _from_cache_t(dil_out[0]), jnp.stack(dil_p[1]), _from_cache_t(dil_out[1]),
            jnp.stack(dil_p[2]), _from_cache_t(dil_out[2]))
```

```python
import functools

import numpy as np
import jax
import jax.numpy as jnp
from jax import lax
from jax.experimental import pallas as pl
from jax.experimental.pallas import tpu as pltpu

F32 = jnp.float32
BF16 = jnp.bfloat16

D_MODEL = 1024
HEAD_DIM = 64
LRU_WIDTH = 512
LRU_C = 8.0
CONV_W = 4
SWA_HEADS = 8
SWA_KV_HEADS = 2
SWA_GROUP = SWA_HEADS // SWA_KV_HEADS
SWA_WINDOW = 128
DIL_WINDOWS = (128, 512, 2048)
DIL_RATES = (1, 4, 16)
N_DIL = 3
DIL_HEADS = 8
BAND = 128
N_EXPERTS = 64
N_EXPERT_GROUPS = 8
GROUP_SIZE = N_EXPERTS // N_EXPERT_GROUPS
TOPK_GROUPS = 4
TOP_K = 8
D_EXPERT = 256
ROUTED_SCALE = 2.5
LN_EPS = 1e-5
DEPTH = 4
DEEPNORM_ALPHA = (2 * DEPTH) ** 0.25

LANES = 128
SUBLANES = 8
ROW_TILES = D_MODEL // LANES
VMEM_LIMIT = 48 * 1024 * 1024

NEG_INF = float("-inf")


def _cparams(n_grid_dims):
    return pltpu.CompilerParams(dimension_semantics=("arbitrary",) * n_grid_dims, vmem_limit_bytes=VMEM_LIMIT)


def _alibi_slopes(n):
    return np.exp2(-8.0 * np.arange(1, n + 1, dtype=np.float64) / n)


def _bdot(a, b):
    return jnp.dot(a.astype(BF16), b.astype(BF16), preferred_element_type=F32)


def _layer_norm(z, g, b):
    zc = z - jnp.mean(z, -1, keepdims=True)
    var = jnp.mean(zc * zc, -1, keepdims=True)
    return zc * lax.rsqrt(var + LN_EPS) * g + b


def _store_tile_rows(ref, val, rows):
    for c in range(ROW_TILES):
        ref[pl.ds(c, rows, stride=ROW_TILES), :] = val[:, c * LANES:(c + 1) * LANES]


def _load_tile_rows(ref, rows):
    return jnp.concatenate([ref[pl.ds(c, rows, stride=ROW_TILES), :] for c in range(ROW_TILES)], axis=-1)


def _mm_kernel(x_ref, w_ref, o_ref):
    o_ref[...] = _bdot(x_ref[...], w_ref[...])


def _matmul(x, w, tm, tn, name):
    m, k = x.shape
    n = w.shape[1]
    return pl.pallas_call(
        _mm_kernel,
        grid=(m // tm, n // tn),
        in_specs=[pl.BlockSpec((tm, k), lambda i, j: (i, 0)), pl.BlockSpec((k, tn), lambda i, j: (0, j))],
        out_specs=pl.BlockSpec((tm, tn), lambda i, j: (i, j)),
        out_shape=jax.ShapeDtypeStruct((m, n), F32),
        compiler_params=_cparams(2),
        name=name,
    )(x, w)


def _outproj_even_kernel(x_ref, a_ref, o_ref, w_ref, g_ref, b_ref, y_ref, yt_ref, wbf_ref):
    @pl.when(pl.program_id(0) == 0)
    def _():
        wbf_ref[...] = w_ref[...].astype(BF16)

    half = a_ref.shape[1]
    f = jnp.dot(a_ref[...].astype(BF16), wbf_ref[0:half, :], preferred_element_type=F32)
    f = f + jnp.dot(o_ref[...].astype(BF16), wbf_ref[half:, :], preferred_element_type=F32)
    y = _layer_norm(DEEPNORM_ALPHA * x_ref[...] + f, g_ref[...], b_ref[...])
    y_ref[...] = y
    _store_tile_rows(yt_ref, y, y.shape[0])


def _outproj_even(x, a, o, w, g, b, tm):
    m = x.shape[0]
    row = lambda i: (i, 0)
    fixed = lambda i: (0, 0)
    return pl.pallas_call(
        _outproj_even_kernel,
        grid=(m // tm,),
        in_specs=[pl.BlockSpec((tm, D_MODEL), row), pl.BlockSpec((tm, a.shape[1]), row),
                  pl.BlockSpec((tm, o.shape[1]), row), pl.BlockSpec(w.shape, fixed),
                  pl.BlockSpec((1, D_MODEL), fixed), pl.BlockSpec((1, D_MODEL), fixed)],
        out_specs=[pl.BlockSpec((tm, D_MODEL), row), pl.BlockSpec((tm * ROW_TILES, LANES), row)],
        out_shape=[jax.ShapeDtypeStruct((m, D_MODEL), F32), jax.ShapeDtypeStruct((m * ROW_TILES, LANES), F32)],
        scratch_shapes=[pltpu.VMEM(w.shape, BF16)],
        compiler_params=_cparams(1),
        name="outproj_even",
    )(x, a, o, w, g.reshape(1, -1), b.reshape(1, -1))


def _outproj_odd_kernel(x_ref, o1_ref, o2_ref, o3_ref, l1_ref, l2_ref, l3_ref, e1_ref, e2_ref, e3_ref, w_ref, g_ref,
                        b_ref, y_ref, yt_ref, wbf_ref):
    @pl.when(pl.program_id(0) == 0)
    def _():
        wbf_ref[...] = w_ref[...].astype(BF16)

    expand = lambda l_ref, e_ref: jnp.dot(l_ref[...], e_ref[...], preferred_element_type=F32,
                                          precision=lax.Precision.HIGHEST)
    l1, l2, l3 = expand(l1_ref, e1_ref), expand(l2_ref, e2_ref), expand(l3_ref, e3_ref)
    m = jnp.maximum(jnp.maximum(l1, l2), l3)
    w1, w2, w3 = jnp.exp(l1 - m), jnp.exp(l2 - m), jnp.exp(l3 - m)
    mix = (w1 * o1_ref[...] + w2 * o2_ref[...] + w3 * o3_ref[...]) / (w1 + w2 + w3)
    f = jnp.dot(mix.astype(BF16), wbf_ref[...], preferred_element_type=F32)
    y = _layer_norm(DEEPNORM_ALPHA * x_ref[...] + f, g_ref[...], b_ref[...])
    y_ref[...] = y
    _store_tile_rows(yt_ref, y, y.shape[0])


def _head_expand_matrix(n_heads, heads_per_tile):
    n_tiles = n_heads // heads_per_tile
    et = np.zeros((n_tiles * LANES, n_heads * HEAD_DIM), np.float32)
    for h in range(n_heads):
        et[(h // heads_per_tile) * LANES + h % heads_per_tile, h * HEAD_DIM:(h + 1) * HEAD_DIM] = 1.0
    return jnp.asarray(et)


def _outproj_odd(x, os_, ls_, ets, w, g, b, tm):
    m = x.shape[0]
    row = lambda i: (i, 0)
    fixed = lambda i: (0, 0)
    width = os_[0].shape[1]
    return pl.pallas_call(
        _outproj_odd_kernel,
        grid=(m // tm,),
        in_specs=[pl.BlockSpec((tm, D_MODEL), row)] + [pl.BlockSpec((tm, width), row)] * 3
                 + [pl.BlockSpec((tm, l.shape[1]), row) for l in ls_]
                 + [pl.BlockSpec(e.shape, fixed) for e in ets]
                 + [pl.BlockSpec(w.shape, fixed), pl.BlockSpec((1, D_MODEL), fixed), pl.BlockSpec((1, D_MODEL), fixed)],
        out_specs=[pl.BlockSpec((tm, D_MODEL), row), pl.BlockSpec((tm * ROW_TILES, LANES), row)],
        out_shape=[jax.ShapeDtypeStruct((m, D_MODEL), F32), jax.ShapeDtypeStruct((m * ROW_TILES, LANES), F32)],
        scratch_shapes=[pltpu.VMEM(w.shape, BF16)],
        compiler_params=_cparams(1),
        name="outproj_odd",
    )(x, *os_, *ls_, *ets, w, g.reshape(1, -1), b.reshape(1, -1))


def _band_bias(slopes, dist_scale, window):
    qi = np.arange(BAND)[:, None]
    kj = np.arange(2 * BAND)[None, :]
    dist = qi + BAND - kj
    valid = (dist >= 0) & (dist <= window)
    bias = -slopes[:, None, None] * (dist_scale * dist)[None].astype(np.float64)
    return jnp.asarray(np.where(valid[None], bias, -np.inf).astype(np.float32))


def _band_kernel(*refs, n_heads, group, has_sink):
    if has_sink:
        q_ref, kp_ref, kc_ref, vp_ref, vc_ref, bias_ref, sink_ref, o_ref = refs
    else:
        q_ref, kp_ref, kc_ref, vp_ref, vc_ref, bias_ref, o_ref, lse_ref = refs
    j = pl.program_id(2)
    q = q_ref[0]
    k = jnp.concatenate([kp_ref[0], kc_ref[0]], axis=0)
    v = jnp.concatenate([vp_ref[0], vc_ref[0]], axis=0)
    col = lax.broadcasted_iota(jnp.int32, (BAND, 2 * BAND), 1)
    keep = (col >= BAND) | (j > 0)
    lane = lax.broadcasted_iota(jnp.int32, (BAND, LANES), 1)
    lse_all = jnp.zeros((BAND, LANES), F32)
    for h in range(n_heads):
        kh = h // group
        qh = q[:, h * HEAD_DIM:(h + 1) * HEAD_DIM].astype(BF16)
        kk = k[:, kh * HEAD_DIM:(kh + 1) * HEAD_DIM].astype(BF16)
        vv = v[:, kh * HEAD_DIM:(kh + 1) * HEAD_DIM].astype(BF16)
        s = lax.dot_general(qh, kk, (((1,), (1,)), ((), ())), preferred_element_type=F32)
        s = s * (HEAD_DIM ** -0.5) + bias_ref[h]
        s = jnp.where(keep, s, NEG_INF)
        m = jnp.max(s, axis=-1, keepdims=True)
        e = jnp.exp(s - m)
        l = jnp.sum(e, axis=-1, keepdims=True)
        o = jnp.dot(e.astype(BF16), vv, preferred_element_type=F32) / l
        lse = m + jnp.log(l)
        if has_sink:
            o = o * jax.nn.sigmoid(lse - sink_ref[h])
        else:
            lse_all = jnp.where(lane == h, lse, lse_all)
        o_ref[0, :, h * HEAD_DIM:(h + 1) * HEAD_DIM] = o
    if not has_sink:
        lse_ref[0] = lse_all


def _banded_attention(src, batch, lf, fold, qcol, kcol, vcol, kv_width, n_heads, group, slopes, dist_scale,
                      sinks, name):
    nb = lf // BAND
    qw = n_heads * HEAD_DIM
    bias = _band_bias(slopes, dist_scale, BAND)
    qmap = lambda b, r, j: (b, j, r * qcol[0] + qcol[1])
    kprev = lambda b, r, j: (b, jnp.maximum(j - 1, 0), r * kcol[0] + kcol[1])
    kcur = lambda b, r, j: (b, j, r * kcol[0] + kcol[1])
    vprev = lambda b, r, j: (b, jnp.maximum(j - 1, 0), r * vcol[0] + vcol[1])
    vcur = lambda b, r, j: (b, j, r * vcol[0] + vcol[1])
    in_specs = [pl.BlockSpec((1, BAND, qw), qmap),
                pl.BlockSpec((1, BAND, kv_width), kprev), pl.BlockSpec((1, BAND, kv_width), kcur),
                pl.BlockSpec((1, BAND, kv_width), vprev), pl.BlockSpec((1, BAND, kv_width), vcur),
                pl.BlockSpec(bias.shape, lambda b, r, j: (0, 0, 0))]
    args = [src, src, src, src, src, bias]
    omap = lambda b, r, j: (b, j, r)
    out_specs = [pl.BlockSpec((1, BAND, qw), omap)]
    out_shape = [jax.ShapeDtypeStruct((batch, lf, fold * qw), F32)]
    if sinks is not None:
        in_specs.append(pl.BlockSpec(memory_space=pltpu.SMEM))
        args.append(sinks)
    else:
        out_specs.append(pl.BlockSpec((1, BAND, LANES), omap))
        out_shape.append(jax.ShapeDtypeStruct((batch, lf, fold * LANES), F32))
    return pl.pallas_call(
        functools.partial(_band_kernel, n_heads=n_heads, group=group, has_sink=sinks is not None),
        grid=(batch, fold, nb),
        in_specs=in_specs, out_specs=out_specs, out_shape=out_shape,
        compiler_params=_cparams(3),
        name=name,
    )(*args)


FOLD_HEADS = LANES // HEAD_DIM


def _fold_band_kernel(*refs, dil, has_prev):
    if has_prev:
        q_ref, kp_ref, kc_ref, vp_ref, vc_ref, bias_ref, o_ref, lse_ref, q_s, k_s, v_s, o_s, lse_s = refs
    else:
        q_ref, kc_ref, vc_ref, bias_ref, o_ref, lse_ref, q_s, k_s, v_s, o_s, lse_s = refs
    j = pl.program_id(1)
    hh = pl.program_id(2)
    kr = 2 * BAND if has_prev else BAND
    for rho in range(dil):
        q_s[rho * BAND:(rho + 1) * BAND, :] = q_ref[0, pl.ds(rho, BAND, stride=dil), :]
        if has_prev:
            k_s[rho * kr:rho * kr + BAND, :] = kp_ref[0, pl.ds(rho, BAND, stride=dil), :]
            v_s[rho * kr:rho * kr + BAND, :] = vp_ref[0, pl.ds(rho, BAND, stride=dil), :]
        k_s[rho * kr + kr - BAND:(rho + 1) * kr, :] = kc_ref[0, pl.ds(rho, BAND, stride=dil), :]
        v_s[rho * kr + kr - BAND:(rho + 1) * kr, :] = vc_ref[0, pl.ds(rho, BAND, stride=dil), :]
    col = lax.broadcasted_iota(jnp.int32, (BAND, kr), 1)
    keep = (col >= kr - BAND) | (j > 0)
    lane = lax.broadcasted_iota(jnp.int32, (BAND, LANES), 1)

    def body(rho, carry):
        q = q_s[pl.ds(pl.multiple_of(rho * BAND, BAND), BAND), :]
        k = k_s[pl.ds(pl.multiple_of(rho * kr, BAND), kr), :]
        v = v_s[pl.ds(pl.multiple_of(rho * kr, BAND), kr), :]
        outs = []
        lse_all = jnp.zeros((BAND, LANES), F32)
        for h in range(FOLD_HEADS):
            sl = slice(h * HEAD_DIM, (h + 1) * HEAD_DIM)
            s = lax.dot_general(q[:, sl].astype(BF16), k[:, sl].astype(BF16), (((1,), (1,)), ((), ())),
                                preferred_element_type=F32)
            s = s * (HEAD_DIM ** -0.5) + bias_ref[hh * FOLD_HEADS + h]
            if has_prev:
                s = jnp.where(keep, s, NEG_INF)
            m = jnp.max(s, axis=-1, keepdims=True)
            e = jnp.exp(s - m)
            l = jnp.sum(e, axis=-1, keepdims=True)
            outs.append(jnp.dot(e.astype(BF16), v[:, sl].astype(BF16), preferred_element_type=F32) / l)
            lse_all = jnp.where(lane == h, m + jnp.log(l), lse_all)
        o_s[pl.ds(pl.multiple_of(rho * BAND, BAND), BAND), :] = jnp.concatenate(outs, axis=-1)
        lse_s[pl.ds(pl.multiple_of(rho * BAND, BAND), BAND), :] = lse_all
        return carry

    lax.fori_loop(0, dil, body, 0)
    for rho in range(dil):
        o_ref[0, pl.ds(rho, BAND, stride=dil), :] = o_s[rho * BAND:(rho + 1) * BAND, :]
        lse_ref[0, pl.ds(rho, BAND, stride=dil), :] = lse_s[rho * BAND:(rho + 1) * BAND, :]


def _fold_attention(proj3, group, dil, slopes):
    batch, seq, _ = proj3.shape
    rows = BAND * dil
    nch = seq // rows
    has_prev = nch > 1
    width = FOLD_HEADS * HEAD_DIM
    n_hh = DIL_HEADS // FOLD_HEADS
    base = group * 3 * n_hh
    bias = _band_bias(slopes, dil, BAND)
    if not has_prev:
        bias = bias[:, :, BAND:]
    blk = (1, rows, width)
    cur = lambda part: (lambda b, j, h: (b, j, base + part * n_hh + h))
    prev = lambda part: (lambda b, j, h: (b, jnp.maximum(j - 1, 0), base + part * n_hh + h))
    in_specs = [pl.BlockSpec(blk, cur(0))]
    args = [proj3]
    for part in (1, 2):
        if has_prev:
            in_specs.append(pl.BlockSpec(blk, prev(part)))
            args.append(proj3)
        in_specs.append(pl.BlockSpec(blk, cur(part)))
        args.append(proj3)
    in_specs.append(pl.BlockSpec(bias.shape, lambda b, j, h: (0, 0, 0)))
    args.append(bias)
    kr = 2 * BAND if has_prev else BAND
    omap = lambda b, j, h: (b, j, h)
    return pl.pallas_call(
        functools.partial(_fold_band_kernel, dil=dil, has_prev=has_prev),
        grid=(batch, nch, n_hh),
        in_specs=in_specs,
        out_specs=[pl.BlockSpec(blk, omap), pl.BlockSpec((1, rows, LANES), omap)],
        out_shape=[jax.ShapeDtypeStruct((batch, seq, DIL_HEADS * HEAD_DIM), F32),
                   jax.ShapeDtypeStruct((batch, seq, n_hh * LANES), F32)],
        scratch_shapes=[pltpu.VMEM((rows, width), F32), pltpu.VMEM((dil * kr, width), F32),
                        pltpu.VMEM((dil * kr, width), F32), pltpu.VMEM((rows, width), F32),
                        pltpu.VMEM((rows, LANES), F32)],
        compiler_params=_cparams(3),
        name="dil_prompt_fold",
    )(*args)


def _gelu_tanh(x):
    return 0.5 * x * (1.0 + jnp.tanh(0.7978845608028654 * (x + 0.044715 * x * x * x)))


def _softplus(z):
    return jnp.maximum(z, 0.0) + jnp.log1p(jnp.exp(-jnp.abs(z)))


def _lru_gates(xc, wr, br, wi, bi, lam):
    r = jax.nn.sigmoid(_bdot(xc, wr) + br)
    i = jax.nn.sigmoid(_bdot(xc, wi) + bi)
    log_a = -LRU_C * r * _softplus(-lam)
    a = jnp.exp(log_a)
    th = jnp.tanh(log_a)
    u = jnp.sqrt(-2.0 * th / (1.0 - th)) * (i * xc)
    return a, u


def _lru_prompt_kernel(xr_ref, yg_ref, cw_ref, cb_ref, wr_ref, br_ref, wi_ref, bi_ref, lam_ref,
                       aout_ref, hlast_ref, xpad_ref, a_ref, u_ref):
    length = xr_ref.shape[1]
    xr = xr_ref[0]
    xpad_ref[0:SUBLANES, :] = jnp.zeros((SUBLANES, LANES), F32)
    xpad_ref[SUBLANES:, :] = xr
    cw = cw_ref[...]
    xc = cb_ref[...] + cw[CONV_W - 1:CONV_W, :] * xr
    for j in range(CONV_W - 1):
        shift = CONV_W - 1 - j
        xc = xc + cw[j:j + 1, :] * xpad_ref[pl.ds(SUBLANES - shift, length), :]
    a, u = _lru_gates(xc, wr_ref[0], br_ref[...], wi_ref[0], bi_ref[...], lam_ref[...])
    a_ref[...] = a
    u_ref[...] = u
    row = lax.broadcasted_iota(jnp.int32, (SUBLANES, LANES), 0)

    def body(blk, carry):
        off = pl.multiple_of(blk * SUBLANES, SUBLANES)
        ab = a_ref[pl.ds(off, SUBLANES), :]
        ub = u_ref[pl.ds(off, SUBLANES), :]
        for d in (1, 2, 4):
            ush = jnp.where(row >= d, pltpu.roll(ub, d, 0), 0.0)
            ash = jnp.where(row >= d, pltpu.roll(ab, d, 0), 1.0)
            ub = ab * ush + ub
            ab = ab * ash
        h = ab * carry + ub
        u_ref[pl.ds(off, SUBLANES), :] = h
        return jnp.broadcast_to(h[SUBLANES - 1:SUBLANES, :], (SUBLANES, LANES))

    carry = lax.fori_loop(0, length // SUBLANES, body, jnp.zeros((SUBLANES, LANES), F32), unroll=4)
    hlast_ref[0] = carry[0:1, :]
    aout_ref[0] = u_ref[...] * _gelu_tanh(yg_ref[0])


def _gate_blockdiag(w):
    z = jnp.zeros((4, LANES, LANES), F32)
    z = z.at[:, 0:HEAD_DIM, 0:HEAD_DIM].set(w[0::2])
    z = z.at[:, HEAD_DIM:, HEAD_DIM:].set(w[1::2])
    return z


def _lru_prompt(proj3, conv_w, conv_b, wr_bd, b_r, wi_bd, b_i, lam):
    batch, length, _ = proj3.shape
    n_ct = LRU_WIDTH // LANES
    vec = lambda b, c: (0, c)
    return pl.pallas_call(
        _lru_prompt_kernel,
        grid=(batch, n_ct),
        in_specs=[pl.BlockSpec((1, length, LANES), lambda b, c: (b, 0, c)),
                  pl.BlockSpec((1, length, LANES), lambda b, c: (b, 0, n_ct + c)),
                  pl.BlockSpec((CONV_W, LANES), vec), pl.BlockSpec((1, LANES), vec),
                  pl.BlockSpec((1, LANES, LANES), lambda b, c: (c, 0, 0)), pl.BlockSpec((1, LANES), vec),
                  pl.BlockSpec((1, LANES, LANES), lambda b, c: (c, 0, 0)), pl.BlockSpec((1, LANES), vec),
                  pl.BlockSpec((1, LANES), vec)],
        out_specs=[pl.BlockSpec((1, length, LANES), lambda b, c: (b, 0, c)),
                   pl.BlockSpec((1, 1, LANES), lambda b, c: (b, 0, c))],
        out_shape=[jax.ShapeDtypeStruct((batch, length, LRU_WIDTH), F32),
                   jax.ShapeDtypeStruct((batch, 1, LRU_WIDTH), F32)],
        scratch_shapes=[pltpu.VMEM((length + SUBLANES, LANES), F32), pltpu.VMEM((length, LANES), F32),
                        pltpu.VMEM((length, LANES), F32)],
        compiler_params=_cparams(2),
        name="lru_prompt",
    )(proj3, proj3, conv_w, conv_b.reshape(1, -1), wr_bd, b_r.reshape(1, -1), wi_bd, b_i.reshape(1, -1),
      lam.reshape(1, -1))


def _lru_decode_kernel(xr_ref, yg_ref, cbuf_ref, h0_ref, cw_ref, cb_ref, wr_ref, br_ref, wi_ref, bi_ref, lam_ref,
                       aout_ref, hlast_ref):
    steps = xr_ref.shape[0]
    xs = [cbuf_ref[j] for j in range(CONV_W - 1)] + [xr_ref[t] for t in range(steps)]
    cw = cw_ref[...]
    h = h0_ref[...]
    n_ct = LRU_WIDTH // LANES
    for t in range(steps):
        xc = cb_ref[...]
        for j in range(CONV_W):
            xc = xc + cw[j:j + 1, :] * xs[t + j]
        parts = []
        for c in range(n_ct):
            sl = slice(c * LANES, (c + 1) * LANES)
            parts.append(_lru_gates(xc[:, sl], wr_ref[c], br_ref[:, sl], wi_ref[c], bi_ref[:, sl], lam_ref[:, sl]))
        a = jnp.concatenate([p[0] for p in parts], axis=-1)
        u = jnp.concatenate([p[1] for p in parts], axis=-1)
        h = a * h + u
        aout_ref[t] = h * _gelu_tanh(yg_ref[t])
    hlast_ref[...] = h


def _lru_decode(proj3, cbuf, h0, conv_w, conv_b, wr_bd, b_r, wi_bd, b_i, lam):
    steps, batch, _ = proj3.shape
    full2 = lambda i: (0, 0)
    full3 = lambda i: (0, 0, 0)
    return pl.pallas_call(
        _lru_decode_kernel,
        grid=(1,),
        in_specs=[pl.BlockSpec((steps, batch, LRU_WIDTH), lambda i: (0, 0, 0)),
                  pl.BlockSpec((steps, batch, LRU_WIDTH), lambda i: (0, 0, 1)),
                  pl.BlockSpec(cbuf.shape, full3), pl.BlockSpec(h0.shape, full2),
                  pl.BlockSpec(conv_w.shape, full2), pl.BlockSpec((1, LRU_WIDTH), full2),
                  pl.BlockSpec(wr_bd.shape, full3), pl.BlockSpec((1, LRU_WIDTH), full2),
                  pl.BlockSpec(wi_bd.shape, full3), pl.BlockSpec((1, LRU_WIDTH), full2),
                  pl.BlockSpec((1, LRU_WIDTH), full2)],
        out_specs=[pl.BlockSpec((steps, batch, LRU_WIDTH), full3), pl.BlockSpec((batch, LRU_WIDTH), full2)],
        out_shape=[jax.ShapeDtypeStruct((steps, batch, LRU_WIDTH), F32),
                   jax.ShapeDtypeStruct((batch, LRU_WIDTH), F32)],
        compiler_params=_cparams(1),
        name="lru_decode",
    )(proj3, proj3, cbuf, h0, conv_w, conv_b.reshape(1, -1), wr_bd, b_r.reshape(1, -1), wi_bd,
      b_i.reshape(1, -1), lam.reshape(1, -1))


def _decode_bias(slopes_hq, q_times, window, dil, steps):
    heads, nq = slopes_hq.shape
    bias = np.zeros((heads, nq, window + LANES), np.float64)
    w = np.arange(window)
    for qi, t in enumerate(q_times):
        if t < 0:
            continue
        dist = window + t - w
        ok = (dist % dil == 0) & (dist <= window)
        bias[:, qi, :window] = np.where(ok[None], -slopes_hq[:, qi, None] * dist[None], -np.inf)
        for c in range(LANES):
            d = t - c
            good = c < steps and d >= 0 and d % dil == 0
            bias[:, qi, window + c] = -slopes_hq[:, qi] * d if good else -np.inf
    return jnp.asarray(bias.astype(np.float32))


def _decode_kernel(*refs, window, steps, has_sink, has_alias):
    refs = list(refs)
    q_ref, c_ref, new_ref, tail_ref, bias_ref = refs[:5]
    pos = 5
    sink_ref = None
    if has_sink:
        sink_ref = refs[pos]
        pos += 1
    if has_alias:
        pos += 1
    co_ref, o_ref, lse_ref = refs[pos:pos + 3]
    c = c_ref[0]
    new = new_ref[...]
    bb, _, hb, hd, _ = c.shape
    nq = q_ref.shape[2]
    merge = lambda a: a.reshape((bb * hb,) + a.shape[2:])
    k_ext = jnp.concatenate([merge(c[:, 0]), merge(new[:, 0])], axis=-1).astype(BF16)
    v_ext = jnp.concatenate([merge(c[:, 1]), merge(new[:, 1])], axis=-1).astype(BF16)
    q = merge(q_ref[...]).astype(BF16)
    bias = merge(jnp.broadcast_to(bias_ref[...][None], (bb,) + bias_ref.shape))
    s = jnp.einsum("hqd,hdw->hqw", q, k_ext, preferred_element_type=F32) * (HEAD_DIM ** -0.5) + bias
    m = jnp.max(s, axis=-1, keepdims=True)
    e = jnp.exp(s - m)
    l = jnp.sum(e, axis=-1, keepdims=True)
    lse = m + jnp.log(l)
    p = e / l
    if has_sink:
        sink = merge(jnp.broadcast_to(sink_ref[...][None], (bb,) + sink_ref.shape))
        p = p * jax.nn.sigmoid(lse - sink[:, :, 0:1])
    o = jnp.einsum("hqw,hdw->hqd", p.astype(BF16), v_ext, preferred_element_type=F32)
    o_ref[...] = o.reshape(bb, hb, nq, hd)
    lse_ref[...] = jnp.broadcast_to(lse, (bb * hb, nq, LANES)).reshape(bb, hb, nq, LANES)
    co_ref[0] = pltpu.roll(c, window - steps, 4)
    lane = lax.broadcasted_iota(jnp.int32, tail_ref.shape, 4)
    last = pltpu.roll(c[:, :, :, :, window - LANES:], LANES - steps, 4)
    co_ref[0, :, :, :, :, window - LANES:] = jnp.where(lane >= LANES - steps, tail_ref[...], last)


def _decode_attention(layer, cache_t, prev_out, q, new_front, new_tail, bias, sink, rows_per_step, heads_per_step,
                      steps, name):
    n_layers, batch, _, heads, hd, window = cache_t.shape
    nq = q.shape[2]
    bb, hb = rows_per_step, heads_per_step
    cmap = lambda b, h: (layer, b, 0, h, 0, 0)
    in_specs = [pl.BlockSpec((bb, hb, nq, hd), lambda b, h: (b, h, 0, 0)),
                pl.BlockSpec((1, bb, 2, hb, hd, window), cmap),
                pl.BlockSpec((bb, 2, hb, hd, LANES), lambda b, h: (b, 0, h, 0, 0)),
                pl.BlockSpec((bb, 2, hb, hd, LANES), lambda b, h: (b, 0, h, 0, 0)),
                pl.BlockSpec((hb, nq, window + LANES), lambda b, h: (h, 0, 0))]
    args = [q, cache_t, new_front, new_tail, bias]
    if sink is not None:
        in_specs.append(pl.BlockSpec((hb, nq, LANES), lambda b, h: (h, 0, 0)))
        args.append(sink)
    aliases = {}
    if prev_out is not None:
        in_specs.append(pl.BlockSpec(memory_space=pl.ANY))
        aliases = {len(args): 0}
        args.append(prev_out)
    return pl.pallas_call(
        functools.partial(_decode_kernel, window=window, steps=steps, has_sink=sink is not None,
                          has_alias=prev_out is not None),
        grid=(batch // bb, heads // hb),
        in_specs=in_specs,
        out_specs=[pl.BlockSpec((1, bb, 2, hb, hd, window), cmap),
                   pl.BlockSpec((bb, hb, nq, hd), lambda b, h: (b, h, 0, 0)),
                   pl.BlockSpec((bb, hb, nq, LANES), lambda b, h: (b, h, 0, 0))],
        out_shape=[jax.ShapeDtypeStruct(cache_t.shape, F32),
                   jax.ShapeDtypeStruct((batch, heads, nq, hd), F32),
                   jax.ShapeDtypeStruct((batch, heads, nq, LANES), F32)],
        input_output_aliases=aliases,
        compiler_params=_cparams(2),
        name=name,
    )(*args)


def _router_kernel(x_ref, wt_ref, b_ref, e_ref, g_ref, c_ref):
    logits = lax.dot_general(wt_ref[...].astype(BF16), x_ref[...].astype(BF16), (((1,), (1,)), ((), ())),
                             preferred_element_type=F32)
    scores = jax.nn.sigmoid(logits)
    biased = scores + b_ref[...]
    tm = biased.shape[1]
    b3 = biased.reshape(N_EXPERT_GROUPS, GROUP_SIZE, tm)
    i3 = lax.broadcasted_iota(jnp.int32, b3.shape, 1).astype(F32)
    g1 = jnp.max(b3, axis=1, keepdims=True)
    first = jnp.min(jnp.where(b3 == g1, i3, float(GROUP_SIZE)), axis=1, keepdims=True)
    g2 = jnp.max(jnp.where(i3 == first, NEG_INF, b3), axis=1, keepdims=True)
    gs = g1 + g2
    gi = lax.broadcasted_iota(jnp.int32, gs.shape, 0).astype(F32)
    sel = jnp.zeros(gs.shape, F32)
    for _ in range(TOPK_GROUPS):
        m = jnp.max(gs, axis=0, keepdims=True)
        f = jnp.min(jnp.where(gs == m, gi, float(N_EXPERT_GROUPS)), axis=0, keepdims=True)
        hit = gi == f
        sel = jnp.where(hit, 1.0, sel)
        gs = jnp.where(hit, NEG_INF, gs)
    masked = jnp.where(sel > 0.5, b3, NEG_INF).reshape(N_EXPERTS, tm)
    ei = lax.broadcasted_iota(jnp.int32, masked.shape, 0).astype(F32)
    e_rows, g_rows = [], []
    chosen = jnp.zeros(masked.shape, F32)
    for _ in range(TOP_K):
        m = jnp.max(masked, axis=0, keepdims=True)
        f = jnp.min(jnp.where(masked == m, ei, float(N_EXPERTS)), axis=0, keepdims=True)
        hit = ei == f
        g_rows.append(jnp.sum(jnp.where(hit, scores, 0.0), axis=0, keepdims=True))
        e_rows.append(f)
        chosen = jnp.where(hit, 1.0, chosen)
        masked = jnp.where(hit, NEG_INF, masked)
    gates = jnp.concatenate(g_rows, axis=0)
    gates = gates / jnp.sum(gates, axis=0, keepdims=True) * ROUTED_SCALE
    e_ref[...] = jnp.concatenate(e_rows, axis=0).astype(jnp.int32)
    g_ref[...] = gates
    c_ref[0] = jnp.sum(chosen, axis=1, keepdims=True)


def _router(x, w_router_t, bias, tm):
    n = x.shape[0]
    nt = n // tm
    return pl.pallas_call(
        _router_kernel,
        grid=(nt,),
        in_specs=[pl.BlockSpec((tm, D_MODEL), lambda i: (i, 0)), pl.BlockSpec((N_EXPERTS, D_MODEL), lambda i: (0, 0)),
                  pl.BlockSpec((N_EXPERTS, 1), lambda i: (0, 0))],
        out_specs=[pl.BlockSpec((TOP_K, tm), lambda i: (0, i)), pl.BlockSpec((TOP_K, tm), lambda i: (0, i)),
                   pl.BlockSpec((1, N_EXPERTS, 1), lambda i: (i, 0, 0))],
        out_shape=[jax.ShapeDtypeStruct((TOP_K, n), jnp.int32), jax.ShapeDtypeStruct((TOP_K, n), F32),
                   jax.ShapeDtypeStruct((nt, N_EXPERTS, 1), F32)],
        compiler_params=_cparams(1),
        name="moe_router",
    )(x, w_router_t, bias.reshape(N_EXPERTS, 1))


KEY_EXPERT_SHIFT = 18
KEY_PAD_BIT = 17


def _dispatch_plan(top_e, gates, tile_counts, chunk, tm):
    n = top_e.shape[1]
    n_chunks = n // chunk
    pairs = TOP_K * chunk
    assert pairs < (1 << KEY_PAD_BIT) and tm <= (1 << KEY_PAD_BIT) and pairs % tm == 0
    per_chunk = lambda a: jnp.transpose(a.reshape(TOP_K, n_chunks, chunk), (1, 0, 2)).reshape(n_chunks, pairs)
    te, ga = per_chunk(top_e), per_chunk(gates)
    counts = tile_counts.reshape(n_chunks, -1, N_EXPERTS).sum(axis=1).astype(jnp.int32)
    n_pad = (-counts) % tm
    key_real = (te << KEY_EXPERT_SHIFT) + jnp.arange(pairs, dtype=jnp.int32)[None, :]
    e_pad = jnp.arange(N_EXPERTS, dtype=jnp.int32)[None, :, None]
    j_pad = jnp.arange(tm, dtype=jnp.int32)[None, None, :]
    key_pad = jnp.where(j_pad < n_pad[:, :, None],
                        (e_pad << KEY_EXPERT_SHIFT) + (1 << KEY_PAD_BIT) + j_pad,
                        (N_EXPERTS << KEY_EXPERT_SHIFT) + e_pad * tm + j_pad)
    keys = jnp.concatenate([key_real, key_pad.reshape(n_chunks, N_EXPERTS * tm)], axis=1)
    vals = jnp.concatenate([ga, jnp.zeros((n_chunks, N_EXPERTS * tm), F32)], axis=1)
    keys, vals = lax.sort((keys, vals), dimension=1, num_keys=1)
    e_row = keys >> KEY_EXPERT_SHIFT
    real = ((keys >> KEY_PAD_BIT) & 1) == 0
    real = real & (e_row < N_EXPERTS)
    tok = (keys & ((1 << KEY_PAD_BIT) - 1)) % chunk
    nb = pairs // tm + N_EXPERTS
    gather_idx = jnp.where(real, tok, 0).reshape(n_chunks * nb, 1, tm)
    scatter_idx = jnp.where(real, tok, chunk).reshape(n_chunks * nb, 1, tm)
    scatter_idx = jnp.concatenate([scatter_idx, jnp.full((1, 1, tm), chunk, jnp.int32)], axis=0)
    block_e = jnp.minimum(e_row[:, ::tm], N_EXPERTS - 1).reshape(n_chunks * nb)
    used = ((counts + n_pad).sum(axis=1) // tm).astype(jnp.int32)
    return gather_idx, scatter_idx, vals.reshape(n_chunks * nb * tm, 1), block_e, used, nb


SCATTER_BATCH = 8


def _expert_kernel(be_ref, used_ref, gcur_ref, gnext_ref, scur_ref, sprev_ref, gate_ref, x_hbm, wg_ref, wu_ref,
                   wd_ref, y_hbm, x_s, y_s, xg, obuf, sem, *, tm, chunk):
    c = pl.program_id(0)
    j = pl.program_id(1)
    slot = j % 2
    used = used_ref[c]
    rows = chunk * ROW_TILES

    def gather(idx_ref, s):
        for r in range(tm):
            t = pl.multiple_of(idx_ref[0, 0, r] * ROW_TILES, ROW_TILES)
            xg[s, r * ROW_TILES:(r + 1) * ROW_TILES, :] = x_s[pl.ds(t, ROW_TILES), :]

    def scatter_add(idx_ref, s):
        for r0 in range(0, tm, SCATTER_BATCH):
            offs = [pl.multiple_of(idx_ref[0, 0, r0 + i] * ROW_TILES, ROW_TILES) for i in range(SCATTER_BATCH)]
            sums = [y_s[pl.ds(offs[i], ROW_TILES), :] + obuf[s, (r0 + i) * ROW_TILES:(r0 + i + 1) * ROW_TILES, :]
                    for i in range(SCATTER_BATCH)]
            for i in range(SCATTER_BATCH):
                y_s[pl.ds(offs[i], ROW_TILES), :] = sums[i]

    @pl.when(j == 0)
    def _():
        load = pltpu.make_async_copy(x_hbm.at[pl.ds(c * rows, rows)], x_s, sem.at[0])
        load.start()
        y_s[...] = jnp.zeros(y_s.shape, F32)

        @pl.when(c == 0)
        def _():
            obuf[...] = jnp.zeros(obuf.shape, F32)

        load.wait()
        gather(gcur_ref, slot)

    @pl.when(j < used)
    def _():
        gather(gnext_ref, 1 - slot)
        x = _load_tile_rows(xg.at[slot], tm).astype(BF16)
        hg = jnp.dot(x, wg_ref[0].astype(BF16), preferred_element_type=F32)
        hu = jnp.dot(x, wu_ref[0].astype(BF16), preferred_element_type=F32)
        hid = (hg * jax.nn.sigmoid(hg)) * hu
        out = jnp.dot(hid.astype(BF16), wd_ref[0].astype(BF16), preferred_element_type=F32) * gate_ref[...]
        _store_tile_rows(obuf.at[slot], out, tm)
        scatter_add(sprev_ref, 1 - slot)

    @pl.when(j == used - 1)
    def _():
        scatter_add(scur_ref, slot)
        store = pltpu.make_async_copy(y_s.at[pl.ds(0, rows)], y_hbm.at[pl.ds(c * rows, rows)], sem.at[1])
        store.start()
        store.wait()


def _experts(x_tiles, plan, w_gate, w_up, w_down, n_tokens, chunk, tm):
    gather_idx, scatter_idx, row_gate, block_e, used, nb = plan
    n_chunks = n_tokens // chunk
    dump_block = n_chunks * nb
    smem_blk = lambda f: pl.BlockSpec((1, 1, tm), f, memory_space=pltpu.SMEM)
    wmap = lambda c, j, be, us: (be[c * nb + j], 0, 0)
    grid_spec = pltpu.PrefetchScalarGridSpec(
        num_scalar_prefetch=2,
        grid=(n_chunks, nb),
        in_specs=[smem_blk(lambda c, j, be, us: (c * nb + j, 0, 0)),
                  smem_blk(lambda c, j, be, us: (c * nb + jnp.minimum(j + 1, nb - 1), 0, 0)),
                  smem_blk(lambda c, j, be, us: (c * nb + j, 0, 0)),
                  smem_blk(lambda c, j, be, us: (jnp.where(j == 0, dump_block, c * nb + j - 1), 0, 0)),
                  pl.BlockSpec((tm, 1), lambda c, j, be, us: (c * nb + j, 0)),
                  pl.BlockSpec(memory_space=pl.ANY),
                  pl.BlockSpec((1, D_MODEL, D_EXPERT), wmap), pl.BlockSpec((1, D_MODEL, D_EXPERT), wmap),
                  pl.BlockSpec((1, D_EXPERT, D_MODEL), wmap)],
        out_specs=pl.BlockSpec(memory_space=pl.ANY),
        scratch_shapes=[pltpu.VMEM((chunk * ROW_TILES, LANES), F32),
                        pltpu.VMEM(((chunk + 1) * ROW_TILES, LANES), F32),
                        pltpu.VMEM((2, tm * ROW_TILES, LANES), F32), pltpu.VMEM((2, tm * ROW_TILES, LANES), F32),
                        pltpu.SemaphoreType.DMA((2,))],
    )
    return pl.pallas_call(
        functools.partial(_expert_kernel, tm=tm, chunk=chunk),
        grid_spec=grid_spec,
        out_shape=jax.ShapeDtypeStruct((n_tokens * ROW_TILES, LANES), F32),
        compiler_params=_cparams(2),
        name="moe_experts",
    )(block_e, used, gather_idx, gather_idx, scatter_idx, scatter_idx, row_gate, x_tiles, w_gate, w_up, w_down)


def _combine_kernel(y_ref, x_ref, wsg_ref, wsu_ref, wsd_ref, g_ref, b_ref, o_ref, wsg_bf, wsu_bf, wsd_bf, *, tm):
    @pl.when(pl.program_id(0) == 0)
    def _():
        wsg_bf[...] = wsg_ref[...].astype(BF16)
        wsu_bf[...] = wsu_ref[...].astype(BF16)
        wsd_bf[...] = wsd_ref[...].astype(BF16)

    routed = _load_tile_rows(y_ref, tm)
    x = x_ref[...]
    xb = x.astype(BF16)
    sg = jnp.dot(xb, wsg_bf[...], preferred_element_type=F32)
    su = jnp.dot(xb, wsu_bf[...], preferred_element_type=F32)
    shared = jnp.dot(((sg * jax.nn.sigmoid(sg)) * su).astype(BF16), wsd_bf[...], preferred_element_type=F32)
    o_ref[...] = _layer_norm(DEEPNORM_ALPHA * x + (routed + shared), g_ref[...], b_ref[...])


def _combine(y_routed, x, ws_gate, ws_up, ws_down, g, b, tm):
    n = x.shape[0]
    fixed = lambda i: (0, 0)
    return pl.pallas_call(
        functools.partial(_combine_kernel, tm=tm),
        grid=(n // tm,),
        in_specs=[pl.BlockSpec((tm * ROW_TILES, LANES), lambda i: (i, 0)), pl.BlockSpec((tm, D_MODEL), lambda i: (i, 0)),
                  pl.BlockSpec(ws_gate.shape, fixed), pl.BlockSpec(ws_up.shape, fixed),
                  pl.BlockSpec(ws_down.shape, fixed),
                  pl.BlockSpec((1, D_MODEL), fixed), pl.BlockSpec((1, D_MODEL), fixed)],
        out_specs=pl.BlockSpec((tm, D_MODEL), lambda i: (i, 0)),
        out_shape=jax.ShapeDtypeStruct((n, D_MODEL), F32),
        scratch_shapes=[pltpu.VMEM(ws_gate.shape, BF16), pltpu.VMEM(ws_up.shape, BF16),
                        pltpu.VMEM(ws_down.shape, BF16)],
        compiler_params=_cparams(1),
        name="moe_combine",
    )(y_routed, x, ws_gate, ws_up, ws_down, g.reshape(1, -1), b.reshape(1, -1))


def _moe_block(x, x_tiles, w_router_t, router_bias, w_gate, w_up, w_down, ws_gate, ws_up, ws_down, g, b,
               tm_route, chunk, tm_expert, tm_combine):
    n = x.shape[0]
    top_e, gates, tile_counts = _router(x, w_router_t, router_bias, tm_route)
    plan = _dispatch_plan(top_e, gates, tile_counts, chunk, tm_expert)
    y_routed = _experts(x_tiles, plan, w_gate, w_up, w_down, n, chunk, tm_expert)
    return _combine(y_routed, x, ws_gate, ws_up, ws_down, g, b, tm_combine)


def _to_cache_t(cache):
    return jnp.transpose(cache, (0, 1, 3, 4, 5, 2))


def _from_cache_t(cache_t):
    return jnp.transpose(cache_t, (0, 1, 5, 2, 3, 4))


def _new_kv_columns(k, v, steps, batch, heads):
    kv = jnp.stack([k, v]).reshape(2, steps, batch, heads, HEAD_DIM)
    kv = jnp.transpose(kv, (2, 0, 3, 4, 1))
    front = jnp.pad(kv, ((0, 0),) * 4 + ((0, LANES - steps),))
    tail = jnp.pad(kv, ((0, 0),) * 4 + ((LANES - steps, 0),))
    return front, tail


def kernel(x_prompt, x_sample, state_lru_h, state_lru_conv, cache_swa_kv, cache_dil1_kv, cache_dil2_kv,
           cache_dil3_kv, w_in_even, conv_w, conv_b, w_rgate, b_rgate, w_igate, b_igate, lru_lambda, swa_sinks,
           w_out_even, w_in_odd, w_out_odd, ln1_g, ln1_b, ln2_g, ln2_b, w_router, router_bias, w_exp_gate,
           w_exp_up, w_exp_down, w_sh_gate, w_sh_up, w_sh_down):
    batch, seq, _ = x_prompt.shape
    dbatch, steps, _ = x_sample.shape
    n_p = batch * seq
    n_s = dbatch * steps
    xp = x_prompt.reshape(n_p, D_MODEL)
    xs = jnp.transpose(x_sample, (1, 0, 2)).reshape(n_s, D_MODEL)

    slopes8 = _alibi_slopes(SWA_HEADS)
    pad_q = SUBLANES - steps
    q_times_dil = list(range(steps)) + [-1] * pad_q
    swa_q_times = [t for t in range(steps) for _ in range(SWA_GROUP)]
    swa_slopes = np.stack([np.tile(slopes8[kv * SWA_GROUP:(kv + 1) * SWA_GROUP], steps)
                           for kv in range(SWA_KV_HEADS)])
    swa_bias = _decode_bias(swa_slopes, swa_q_times, SWA_WINDOW, 1, steps)
    dil_bias = [_decode_bias(np.tile(_alibi_slopes(DIL_HEADS)[:, None], (1, SUBLANES)), q_times_dil,
                             DIL_WINDOWS[g], DIL_RATES[g], steps) for g in range(N_DIL)]

    et_flat = _head_expand_matrix(DIL_HEADS, DIL_HEADS)
    et_fold = _head_expand_matrix(DIL_HEADS, FOLD_HEADS)
    ets_prompt = [et_flat if d == 1 else et_fold for d in DIL_RATES]
    ets_sample = [et_flat] * N_DIL

    swa_t = _to_cache_t(cache_swa_kv)
    dil_t = [_to_cache_t(c) for c in (cache_dil1_kv, cache_dil2_kv, cache_dil3_kv)]
    swa_out = None
    dil_out = [None] * N_DIL
    lru_h_p, lru_h_s, lru_c_p, lru_c_s, swa_p = [], [], [], [], []
    dil_p = [[] for _ in range(N_DIL)]

    for l in range(DEPTH):
        j = l // 2
        if l % 2 == 0:
            wr_bd, wi_bd = _gate_blockdiag(w_rgate[j]), _gate_blockdiag(w_igate[j])
            proj = _matmul(xp, w_in_even[j], 1024, 896, "inproj_even_prompt")
            proj3 = proj.reshape(batch, seq, -1)
            a_out, h_last = _lru_prompt(proj3, conv_w[j], conv_b[j], wr_bd, b_rgate[j], wi_bd, b_igate[j],
                                        lru_lambda[j])
            (o_swa,) = _banded_attention(proj3, batch, seq, 1, (0, 2), (0, 12), (0, 13), SWA_KV_HEADS * HEAD_DIM,
                                         SWA_HEADS, SWA_GROUP, slopes8, 1, swa_sinks[j], "swa_prompt")
            lru_h_p.append(h_last[:, 0])
            lru_c_p.append(proj3[:, seq - (CONV_W - 1):, :LRU_WIDTH])
            keep = min(SWA_WINDOW, seq)
            swa_p.append(proj3[:, seq - keep:, 2 * LRU_WIDTH + SWA_HEADS * HEAD_DIM:]
                         .reshape(batch, keep, 2, SWA_KV_HEADS, HEAD_DIM))
            xp, xp_tiles = _outproj_even(xp, a_out.reshape(n_p, -1), o_swa.reshape(n_p, -1), w_out_even[j],
                                         ln1_g[l], ln1_b[l], 512)
            proj_s = _matmul(xs, w_in_even[j], n_s, 896, "inproj_even_sample")
            proj_s3 = proj_s.reshape(steps, dbatch, -1)
            cbuf = jnp.transpose(state_lru_conv[j], (1, 0, 2))
            a_s, h_s = _lru_decode(proj_s3, cbuf, state_lru_h[j], conv_w[j], conv_b[j], wr_bd, b_rgate[j], wi_bd,
                                   b_igate[j], lru_lambda[j])
            lru_h_s.append(h_s)
            xr_s = proj_s3[:, :, :LRU_WIDTH]
            lru_c_s.append(jnp.transpose(jnp.concatenate([cbuf, xr_s], axis=0)[-(CONV_W - 1):], (1, 0, 2)))
            o0 = 2 * LRU_WIDTH
            q_s = proj_s[:, o0:o0 + SWA_HEADS * HEAD_DIM].reshape(steps, dbatch, SWA_KV_HEADS, SWA_GROUP, HEAD_DIM)
            q_s = jnp.transpose(q_s, (1, 2, 0, 3, 4)).reshape(dbatch, SWA_KV_HEADS, steps * SWA_GROUP, HEAD_DIM)
            o1 = o0 + SWA_HEADS * HEAD_DIM
            o2 = o1 + SWA_KV_HEADS * HEAD_DIM
            front, tail = _new_kv_columns(proj_s[:, o1:o2], proj_s[:, o2:], steps, dbatch, SWA_KV_HEADS)
            sink = jnp.broadcast_to(
                jnp.tile(swa_sinks[j].reshape(SWA_KV_HEADS, SWA_GROUP), (1, steps))[:, :, None],
                (SWA_KV_HEADS, steps * SWA_GROUP, LANES))
            swa_out, o_dec, _ = _decode_attention(j, swa_t, swa_out, q_s, front, tail, swa_bias, sink,
                                                  min(dbatch, 16), SWA_KV_HEADS, steps, "swa_decode")
            o_dec = o_dec.reshape(dbatch, SWA_KV_HEADS, steps, SWA_GROUP, HEAD_DIM)
            o_dec = jnp.transpose(o_dec, (2, 0, 1, 3, 4)).reshape(n_s, SWA_HEADS * HEAD_DIM)
            xs, xs_tiles = _outproj_even(xs, a_s.reshape(n_s, -1), o_dec, w_out_even[j], ln1_g[l], ln1_b[l], n_s)
        else:
            proj = _matmul(xp, w_in_odd[j], 1024, 768, "inproj_odd_prompt")
            row_w = proj.shape[1]
            proj3 = proj.reshape(batch, seq, row_w)
            os_, ls_ = [], []
            for g in range(N_DIL):
                dil = DIL_RATES[g]
                if dil == 1:
                    n_blk = row_w // (DIL_HEADS * HEAD_DIM)
                    o_g, lse_g = _banded_attention(proj3, batch, seq, 1, (n_blk, 3 * g), (n_blk, 3 * g + 1),
                                                   (n_blk, 3 * g + 2), DIL_HEADS * HEAD_DIM, DIL_HEADS, 1,
                                                   _alibi_slopes(DIL_HEADS), dil, None, "dil_prompt")
                else:
                    o_g, lse_g = _fold_attention(proj3, g, dil, _alibi_slopes(DIL_HEADS))
                os_.append(o_g.reshape(n_p, -1))
                ls_.append(lse_g.reshape(n_p, -1))
                keep = min(DIL_WINDOWS[g], seq)
                c0 = g * 3 * DIL_HEADS * HEAD_DIM + DIL_HEADS * HEAD_DIM
                dil_p[g].append(proj3[:, seq - keep:, c0:c0 + 2 * DIL_HEADS * HEAD_DIM]
                                .reshape(batch, keep, 2, DIL_HEADS, HEAD_DIM))
            xp, xp_tiles = _outproj_odd(xp, os_, ls_, ets_prompt, w_out_odd[j], ln1_g[l], ln1_b[l], 512)
            proj_s = _matmul(xs, w_in_odd[j], n_s, 768, "inproj_odd_sample")
            os_, ls_ = [], []
            gw = DIL_HEADS * HEAD_DIM
            for g in range(N_DIL):
                c0 = g * 3 * gw
                q_s = proj_s[:, c0:c0 + gw].reshape(steps, dbatch, DIL_HEADS, HEAD_DIM)
                q_s = jnp.pad(jnp.transpose(q_s, (1, 2, 0, 3)), ((0, 0), (0, 0), (0, pad_q), (0, 0)))
                front, tail = _new_kv_columns(proj_s[:, c0 + gw:c0 + 2 * gw], proj_s[:, c0 + 2 * gw:c0 + 3 * gw],
                                              steps, dbatch, DIL_HEADS)
                rows_per_step, heads_per_step = ((8, 8), (2, 8), (1, 4))[g]
                dil_out[g], o_dec, lse_dec = _decode_attention(j, dil_t[g], dil_out[g], q_s, front, tail,
                                                               dil_bias[g], None, min(dbatch, rows_per_step),
                                                               heads_per_step, steps, "dil_decode")
                o_dec = jnp.transpose(o_dec[:, :, :steps], (2, 0, 1, 3)).reshape(n_s, gw)
                lse_dec = jnp.transpose(lse_dec[:, :, :steps, 0], (2, 0, 1)).reshape(n_s, DIL_HEADS)
                os_.append(o_dec)
                ls_.append(jnp.pad(lse_dec, ((0, 0), (0, LANES - DIL_HEADS))))
            xs, xs_tiles = _outproj_odd(xs, os_, ls_, ets_sample, w_out_odd[j], ln1_g[l], ln1_b[l], n_s)

        moe_w = (jnp.transpose(w_router[l]), router_bias[l], w_exp_gate[l], w_exp_up[l], w_exp_down[l],
                 w_sh_gate[l], w_sh_up[l], w_sh_down[l], ln2_g[l], ln2_b[l])
        xp = _moe_block(xp, xp_tiles, *moe_w, tm_route=512, chunk=min(n_p, 4096), tm_expert=256, tm_combine=512)
        xs = _moe_block(xs, xs_tiles, *moe_w, tm_route=n_s, chunk=n_s, tm_expert=128, tm_combine=n_s)

    y_prompt = xp.reshape(batch, seq, D_MODEL)
    y_sample = jnp.transpose(xs.reshape(steps, dbatch, D_MODEL), (1, 0, 2))
    return (y_prompt, y_sample, jnp.stack(lru_h_p), jnp.stack(lru_h_s), jnp.stack(lru_c_p), jnp.stack(lru_c_s),
            jnp.stack(swa_p), _from_cache_t(swa_out),
            jnp.stack(dil_p[0]), _from_cache_t(dil_out[0]), jnp.stack(dil_p[1]), _from_cache_t(dil_out[1]),
            jnp.stack(dil_p[2]), _from_cache_t(dil_out[2]))
```

```python
import functools

import numpy as np
import jax
import jax.numpy as jnp
from jax import lax
from jax.experimental import pallas as pl
from jax.experimental.pallas import tpu as pltpu

F32 = jnp.float32
BF16 = jnp.bfloat16

D_MODEL = 1024
HEAD_DIM = 64
LRU_WIDTH = 512
LRU_C = 8.0
CONV_W = 4
SWA_HEADS = 8
SWA_KV_HEADS = 2
SWA_GROUP = SWA_HEADS // SWA_KV_HEADS
SWA_WINDOW = 128
DIL_WINDOWS = (128, 512, 2048)
DIL_RATES = (1, 4, 16)
N_DIL = 3
DIL_HEADS = 8
BAND = 128
N_EXPERTS = 64
N_EXPERT_GROUPS = 8
GROUP_SIZE = N_EXPERTS // N_EXPERT_GROUPS
TOPK_GROUPS = 4
TOP_K = 8
D_EXPERT = 256
ROUTED_SCALE = 2.5
LN_EPS = 1e-5
DEPTH = 4
DEEPNORM_ALPHA = (2 * DEPTH) ** 0.25

LANES = 128
SUBLANES = 8
ROW_TILES = D_MODEL // LANES
VMEM_LIMIT = 48 * 1024 * 1024

NEG_INF = float("-inf")


def _cparams(n_grid_dims):
    return pltpu.CompilerParams(dimension_semantics=("arbitrary",) * n_grid_dims, vmem_limit_bytes=VMEM_LIMIT)


def _alibi_slopes(n):
    return np.exp2(-8.0 * np.arange(1, n + 1, dtype=np.float64) / n)


def _bdot(a, b):
    return jnp.dot(a.astype(BF16), b.astype(BF16), preferred_element_type=F32)


def _layer_norm(z, g, b):
    zc = z - jnp.mean(z, -1, keepdims=True)
    var = jnp.mean(zc * zc, -1, keepdims=True)
    return zc * lax.rsqrt(var + LN_EPS) * g + b


def _store_tile_rows(ref, val, rows):
    for c in range(ROW_TILES):
        ref[pl.ds(c, rows, stride=ROW_TILES), :] = val[:, c * LANES:(c + 1) * LANES]


def _load_tile_rows(ref, rows):
    return jnp.concatenate([ref[pl.ds(c, rows, stride=ROW_TILES), :] for c in range(ROW_TILES)], axis=-1)


def _mm_kernel(x_ref, w_ref, o_ref):
    o_ref[...] = _bdot(x_ref[...], w_ref[...])


def _matmul(x, w, tm, tn, name):
    m, k = x.shape
    n = w.shape[1]
    return pl.pallas_call(
        _mm_kernel,
        grid=(m // tm, n // tn),
        in_specs=[pl.BlockSpec((tm, k), lambda i, j: (i, 0)), pl.BlockSpec((k, tn), lambda i, j: (0, j))],
        out_specs=pl.BlockSpec((tm, tn), lambda i, j: (i, j)),
        out_shape=jax.ShapeDtypeStruct((m, n), F32),
        compiler_params=_cparams(2),
        name=name,
    )(x, w)


def _outproj_even_kernel(x_ref, a_ref, o_ref, w_ref, g_ref, b_ref, y_ref, yt_ref, wbf_ref):
    @pl.when(pl.program_id(0) == 0)
    def _():
        wbf_ref[...] = w_ref[...].astype(BF16)

    half = a_ref.shape[1]
    f = jnp.dot(a_ref[...].astype(BF16), wbf_ref[0:half, :], preferred_element_type=F32)
    f = f + jnp.dot(o_ref[...].astype(BF16), wbf_ref[half:, :], preferred_element_type=F32)
    y = _layer_norm(DEEPNORM_ALPHA * x_ref[...] + f, g_ref[...], b_ref[...])
    y_ref[...] = y
    _store_tile_rows(yt_ref, y, y.shape[0])


def _outproj_even(x, a, o, w, g, b, tm):
    m = x.shape[0]
    row = lambda i: (i, 0)
    fixed = lambda i: (0, 0)
    return pl.pallas_call(
        _outproj_even_kernel,
        grid=(m // tm,),
        in_specs=[pl.BlockSpec((tm, D_MODEL), row), pl.BlockSpec((tm, a.shape[1]), row),
                  pl.BlockSpec((tm, o.shape[1]), row), pl.BlockSpec(w.shape, fixed),
                  pl.BlockSpec((1, D_MODEL), fixed), pl.BlockSpec((1, D_MODEL), fixed)],
        out_specs=[pl.BlockSpec((tm, D_MODEL), row), pl.BlockSpec((tm * ROW_TILES, LANES), row)],
        out_shape=[jax.ShapeDtypeStruct((m, D_MODEL), F32), jax.ShapeDtypeStruct((m * ROW_TILES, LANES), F32)],
        scratch_shapes=[pltpu.VMEM(w.shape, BF16)],
        compiler_params=_cparams(1),
        name="outproj_even",
    )(x, a, o, w, g.reshape(1, -1), b.reshape(1, -1))


def _outproj_odd_kernel(x_ref, o1_ref, o2_ref, o3_ref, l1_ref, l2_ref, l3_ref, e1_ref, e2_ref, e3_ref, w_ref, g_ref,
                        b_ref, y_ref, yt_ref, wbf_ref):
    @pl.when(pl.program_id(0) == 0)
    def _():
        wbf_ref[...] = w_ref[...].astype(BF16)

    def expand(l_ref, e_ref):
        lv = l_ref[...]
        hi = lv.astype(BF16)
        r1 = lv - hi.astype(F32)
        mid = r1.astype(BF16)
        lo = (r1 - mid.astype(F32)).astype(BF16)
        e = e_ref[...]
        dot = lambda a: jnp.dot(a, e, preferred_element_type=F32)
        return (dot(hi) + dot(mid)) + dot(lo)

    l1, l2, l3 = expand(l1_ref, e1_ref), expand(l2_ref, e2_ref), expand(l3_ref, e3_ref)
    m = jnp.maximum(jnp.maximum(l1, l2), l3)
    w1, w2, w3 = jnp.exp(l1 - m), jnp.exp(l2 - m), jnp.exp(l3 - m)
    mix = (w1 * o1_ref[...] + w2 * o2_ref[...] + w3 * o3_ref[...]) / (w1 + w2 + w3)
    f = jnp.dot(mix.astype(BF16), wbf_ref[...], preferred_element_type=F32)
    y = _layer_norm(DEEPNORM_ALPHA * x_ref[...] + f, g_ref[...], b_ref[...])
    y_ref[...] = y
    _store_tile_rows(yt_ref, y, y.shape[0])


def _head_expand_matrix(n_heads, heads_per_tile):
    n_tiles = n_heads // heads_per_tile
    et = np.zeros((n_tiles * LANES, n_heads * HEAD_DIM), np.float32)
    for h in range(n_heads):
        et[(h // heads_per_tile) * LANES + h % heads_per_tile, h * HEAD_DIM:(h + 1) * HEAD_DIM] = 1.0
    return jnp.asarray(et, dtype=BF16)


def _outproj_odd(x, os_, ls_, ets, w, g, b, tm):
    m = x.shape[0]
    row = lambda i: (i, 0)
    fixed = lambda i: (0, 0)
    width = os_[0].shape[1]
    return pl.pallas_call(
        _outproj_odd_kernel,
        grid=(m // tm,),
        in_specs=[pl.BlockSpec((tm, D_MODEL), row)] + [pl.BlockSpec((tm, width), row)] * 3
                 + [pl.BlockSpec((tm, l.shape[1]), row) for l in ls_]
                 + [pl.BlockSpec(e.shape, fixed) for e in ets]
                 + [pl.BlockSpec(w.shape, fixed), pl.BlockSpec((1, D_MODEL), fixed), pl.BlockSpec((1, D_MODEL), fixed)],
        out_specs=[pl.BlockSpec((tm, D_MODEL), row), pl.BlockSpec((tm * ROW_TILES, LANES), row)],
        out_shape=[jax.ShapeDtypeStruct((m, D_MODEL), F32), jax.ShapeDtypeStruct((m * ROW_TILES, LANES), F32)],
        scratch_shapes=[pltpu.VMEM(w.shape, BF16)],
        compiler_params=_cparams(1),
        name="outproj_odd",
    )(x, *os_, *ls_, *ets, w, g.reshape(1, -1), b.reshape(1, -1))


def _band_bias(slopes, dist_scale, window):
    qi = np.arange(BAND)[:, None]
    kj = np.arange(2 * BAND)[None, :]
    dist = qi + BAND - kj
    valid = (dist >= 0) & (dist <= window)
    bias = -slopes[:, None, None] * (dist_scale * dist)[None].astype(np.float64)
    return jnp.asarray(np.where(valid[None], bias, -np.inf).astype(np.float32))


def _band_kernel(*refs, n_heads, group, has_sink):
    if has_sink:
        q_ref, kp_ref, kc_ref, vp_ref, vc_ref, bias_ref, sink_ref, o_ref = refs
    else:
        q_ref, kp_ref, kc_ref, vp_ref, vc_ref, bias_ref, o_ref, lse_ref = refs
    j = pl.program_id(2)
    q = q_ref[0]
    k = jnp.concatenate([kp_ref[0], kc_ref[0]], axis=0)
    v = jnp.concatenate([vp_ref[0], vc_ref[0]], axis=0)
    col = lax.broadcasted_iota(jnp.int32, (BAND, 2 * BAND), 1)
    keep = (col >= BAND) | (j > 0)
    lane = lax.broadcasted_iota(jnp.int32, (BAND, LANES), 1)
    lse_all = jnp.zeros((BAND, LANES), F32)
    for h in range(n_heads):
        kh = h // group
        qh = q[:, h * HEAD_DIM:(h + 1) * HEAD_DIM].astype(BF16)
        kk = k[:, kh * HEAD_DIM:(kh + 1) * HEAD_DIM].astype(BF16)
        vv = v[:, kh * HEAD_DIM:(kh + 1) * HEAD_DIM].astype(BF16)
        s = lax.dot_general(qh, kk, (((1,), (1,)), ((), ())), preferred_element_type=F32)
        s = s * (HEAD_DIM ** -0.5) + bias_ref[h]
        s = jnp.where(keep, s, NEG_INF)
        m = jnp.max(s, axis=-1, keepdims=True)
        e = jnp.exp(s - m)
        l = jnp.sum(e, axis=-1, keepdims=True)
        o = jnp.dot(e.astype(BF16), vv, preferred_element_type=F32) / l
        lse = m + jnp.log(l)
        if has_sink:
            o = o * jax.nn.sigmoid(lse - sink_ref[h])
        else:
            lse_all = jnp.where(lane == h, lse, lse_all)
        o_ref[0, :, h * HEAD_DIM:(h + 1) * HEAD_DIM] = o
    if not has_sink:
        lse_ref[0] = lse_all


def _banded_attention(src, batch, lf, fold, qcol, kcol, vcol, kv_width, n_heads, group, slopes, dist_scale,
                      sinks, name):
    nb = lf // BAND
    qw = n_heads * HEAD_DIM
    bias = _band_bias(slopes, dist_scale, BAND)
    qmap = lambda b, r, j: (b, j, r * qcol[0] + qcol[1])
    kprev = lambda b, r, j: (b, jnp.maximum(j - 1, 0), r * kcol[0] + kcol[1])
    kcur = lambda b, r, j: (b, j, r * kcol[0] + kcol[1])
    vprev = lambda b, r, j: (b, jnp.maximum(j - 1, 0), r * vcol[0] + vcol[1])
    vcur = lambda b, r, j: (b, j, r * vcol[0] + vcol[1])
    in_specs = [pl.BlockSpec((1, BAND, qw), qmap),
                pl.BlockSpec((1, BAND, kv_width), kprev), pl.BlockSpec((1, BAND, kv_width), kcur),
                pl.BlockSpec((1, BAND, kv_width), vprev), pl.BlockSpec((1, BAND, kv_width), vcur),
                pl.BlockSpec(bias.shape, lambda b, r, j: (0, 0, 0))]
    args = [src, src, src, src, src, bias]
    omap = lambda b, r, j: (b, j, r)
    out_specs = [pl.BlockSpec((1, BAND, qw), omap)]
    out_shape = [jax.ShapeDtypeStruct((batch, lf, fold * qw), F32)]
    if sinks is not None:
        in_specs.append(pl.BlockSpec(memory_space=pltpu.SMEM))
        args.append(sinks)
    else:
        out_specs.append(pl.BlockSpec((1, BAND, LANES), omap))
        out_shape.append(jax.ShapeDtypeStruct((batch, lf, fold * LANES), F32))
    return pl.pallas_call(
        functools.partial(_band_kernel, n_heads=n_heads, group=group, has_sink=sinks is not None),
        grid=(batch, fold, nb),
        in_specs=in_specs, out_specs=out_specs, out_shape=out_shape,
        compiler_params=_cparams(3),
        name=name,
    )(*args)


FOLD_HEADS = LANES // HEAD_DIM


def _fold_band_kernel(*refs, dil, has_prev):
    if has_prev:
        q_ref, kp_ref, kc_ref, vp_ref, vc_ref, bias_ref, o_ref, lse_ref, q_s, k_s, v_s, o_s, lse_s = refs
    else:
        q_ref, kc_ref, vc_ref, bias_ref, o_ref, lse_ref, q_s, k_s, v_s, o_s, lse_s = refs
    j = pl.program_id(1)
    hh = pl.program_id(2)
    kr = 2 * BAND if has_prev else BAND
    for rho in range(dil):
        q_s[rho * BAND:(rho + 1) * BAND, :] = q_ref[0, pl.ds(rho, BAND, stride=dil), :]
        if has_prev:
            k_s[rho * kr:rho * kr + BAND, :] = kp_ref[0, pl.ds(rho, BAND, stride=dil), :]
            v_s[rho * kr:rho * kr + BAND, :] = vp_ref[0, pl.ds(rho, BAND, stride=dil), :]
        k_s[rho * kr + kr - BAND:(rho + 1) * kr, :] = kc_ref[0, pl.ds(rho, BAND, stride=dil), :]
        v_s[rho * kr + kr - BAND:(rho + 1) * kr, :] = vc_ref[0, pl.ds(rho, BAND, stride=dil), :]
    col = lax.broadcasted_iota(jnp.int32, (BAND, kr), 1)
    keep = (col >= kr - BAND) | (j > 0)
    lane = lax.broadcasted_iota(jnp.int32, (BAND, LANES), 1)

    def body(rho, carry):
        q = q_s[pl.ds(pl.multiple_of(rho * BAND, BAND), BAND), :]
        k = k_s[pl.ds(pl.multiple_of(rho * kr, BAND), kr), :]
        v = v_s[pl.ds(pl.multiple_of(rho * kr, BAND), kr), :]
        outs = []
        lse_all = jnp.zeros((BAND, LANES), F32)
        for h in range(FOLD_HEADS):
            sl = slice(h * HEAD_DIM, (h + 1) * HEAD_DIM)
            s = lax.dot_general(q[:, sl].astype(BF16), k[:, sl].astype(BF16), (((1,), (1,)), ((), ())),
                                preferred_element_type=F32)
            s = s * (HEAD_DIM ** -0.5) + bias_ref[hh * FOLD_HEADS + h]
            if has_prev:
                s = jnp.where(keep, s, NEG_INF)
            m = jnp.max(s, axis=-1, keepdims=True)
            e = jnp.exp(s - m)
            l = jnp.sum(e, axis=-1, keepdims=True)
            outs.append(jnp.dot(e.astype(BF16), v[:, sl].astype(BF16), preferred_element_type=F32) / l)
            lse_all = jnp.where(lane == h, m + jnp.log(l), lse_all)
        o_s[pl.ds(pl.multiple_of(rho * BAND, BAND), BAND), :] = jnp.concatenate(outs, axis=-1)
        lse_s[pl.ds(pl.multiple_of(rho * BAND, BAND), BAND), :] = lse_all
        return carry

    lax.fori_loop(0, dil, body, 0, unroll=2)
    for rho in range(dil):
        o_ref[0, pl.ds(rho, BAND, stride=dil), :] = o_s[rho * BAND:(rho + 1) * BAND, :]
        lse_ref[0, pl.ds(rho, BAND, stride=dil), :] = lse_s[rho * BAND:(rho + 1) * BAND, :]


def _fold_attention(proj3, group, dil, slopes):
    batch, seq, _ = proj3.shape
    rows = BAND * dil
    nch = seq // rows
    has_prev = nch > 1
    width = FOLD_HEADS * HEAD_DIM
    n_hh = DIL_HEADS // FOLD_HEADS
    base = group * 3 * n_hh
    bias = _band_bias(slopes, dil, BAND)
    if not has_prev:
        bias = bias[:, :, BAND:]
    blk = (1, rows, width)
    cur = lambda part: (lambda b, j, h: (b, j, base + part * n_hh + h))
    prev = lambda part: (lambda b, j, h: (b, jnp.maximum(j - 1, 0), base + part * n_hh + h))
    in_specs = [pl.BlockSpec(blk, cur(0))]
    args = [proj3]
    for part in (1, 2):
        if has_prev:
            in_specs.append(pl.BlockSpec(blk, prev(part)))
            args.append(proj3)
        in_specs.append(pl.BlockSpec(blk, cur(part)))
        args.append(proj3)
    in_specs.append(pl.BlockSpec(bias.shape, lambda b, j, h: (0, 0, 0)))
    args.append(bias)
    kr = 2 * BAND if has_prev else BAND
    omap = lambda b, j, h: (b, j, h)
    return pl.pallas_call(
        functools.partial(_fold_band_kernel, dil=dil, has_prev=has_prev),
        grid=(batch, nch, n_hh),
        in_specs=in_specs,
        out_specs=[pl.BlockSpec(blk, omap), pl.BlockSpec((1, rows, LANES), omap)],
        out_shape=[jax.ShapeDtypeStruct((batch, seq, DIL_HEADS * HEAD_DIM), F32),
                   jax.ShapeDtypeStruct((batch, seq, n_hh * LANES), F32)],
        scratch_shapes=[pltpu.VMEM((rows, width), F32), pltpu.VMEM((dil * kr, width), F32),
                        pltpu.VMEM((dil * kr, width), F32), pltpu.VMEM((rows, width), F32),
                        pltpu.VMEM((rows, LANES), F32)],
        compiler_params=_cparams(3),
        name="dil_prompt_fold",
    )(*args)


def _gelu_tanh(x):
    return 0.5 * x * (1.0 + jnp.tanh(0.7978845608028654 * (x + 0.044715 * x * x * x)))


def _softplus(z):
    return jnp.maximum(z, 0.0) + jnp.log1p(jnp.exp(-jnp.abs(z)))


def _lru_gates(xc, wr, br, wi, bi, lam):
    r = jax.nn.sigmoid(_bdot(xc, wr) + br)
    i = jax.nn.sigmoid(_bdot(xc, wi) + bi)
    log_a = -LRU_C * r * _softplus(-lam)
    a = jnp.exp(log_a)
    th = jnp.tanh(log_a)
    u = jnp.sqrt(-2.0 * th / (1.0 - th)) * (i * xc)
    return a, u


def _lru_prompt_kernel(xr_ref, yg_ref, cw_ref, cb_ref, wr_ref, br_ref, wi_ref, bi_ref, lam_ref,
                       aout_ref, hlast_ref, xpad_ref, a_ref, u_ref):
    length = xr_ref.shape[1]
    xr = xr_ref[0]
    xpad_ref[0:SUBLANES, :] = jnp.zeros((SUBLANES, LANES), F32)
    xpad_ref[SUBLANES:, :] = xr
    cw = cw_ref[...]
    xc = cb_ref[...] + cw[CONV_W - 1:CONV_W, :] * xr
    for j in range(CONV_W - 1):
        shift = CONV_W - 1 - j
        xc = xc + cw[j:j + 1, :] * xpad_ref[pl.ds(SUBLANES - shift, length), :]
    a, u = _lru_gates(xc, wr_ref[0], br_ref[...], wi_ref[0], bi_ref[...], lam_ref[...])
    a_ref[...] = a
    u_ref[...] = u
    row = lax.broadcasted_iota(jnp.int32, (SUBLANES, LANES), 0)

    def body(blk, carry):
        off = pl.multiple_of(blk * SUBLANES, SUBLANES)
        ab = a_ref[pl.ds(off, SUBLANES), :]
        ub = u_ref[pl.ds(off, SUBLANES), :]
        for d in (1, 2, 4):
            ush = jnp.where(row >= d, pltpu.roll(ub, d, 0), 0.0)
            ash = jnp.where(row >= d, pltpu.roll(ab, d, 0), 1.0)
            ub = ab * ush + ub
            ab = ab * ash
        h = ab * carry + ub
        u_ref[pl.ds(off, SUBLANES), :] = h
        return jnp.broadcast_to(h[SUBLANES - 1:SUBLANES, :], (SUBLANES, LANES))

    carry = lax.fori_loop(0, length // SUBLANES, body, jnp.zeros((SUBLANES, LANES), F32), unroll=4)
    hlast_ref[0] = carry[0:1, :]
    aout_ref[0] = u_ref[...] * _gelu_tanh(yg_ref[0])


def _gate_blockdiag(w):
    z = jnp.zeros((4, LANES, LANES), F32)
    z = z.at[:, 0:HEAD_DIM, 0:HEAD_DIM].set(w[0::2])
    z = z.at[:, HEAD_DIM:, HEAD_DIM:].set(w[1::2])
    return z


def _lru_prompt(proj3, conv_w, conv_b, wr_bd, b_r, wi_bd, b_i, lam):
    batch, length, _ = proj3.shape
    n_ct = LRU_WIDTH // LANES
    vec = lambda b, c: (0, c)
    return pl.pallas_call(
        _lru_prompt_kernel,
        grid=(batch, n_ct),
        in_specs=[pl.BlockSpec((1, length, LANES), lambda b, c: (b, 0, c)),
                  pl.BlockSpec((1, length, LANES), lambda b, c: (b, 0, n_ct + c)),
                  pl.BlockSpec((CONV_W, LANES), vec), pl.BlockSpec((1, LANES), vec),
                  pl.BlockSpec((1, LANES, LANES), lambda b, c: (c, 0, 0)), pl.BlockSpec((1, LANES), vec),
                  pl.BlockSpec((1, LANES, LANES), lambda b, c: (c, 0, 0)), pl.BlockSpec((1, LANES), vec),
                  pl.BlockSpec((1, LANES), vec)],
        out_specs=[pl.BlockSpec((1, length, LANES), lambda b, c: (b, 0, c)),
                   pl.BlockSpec((1, 1, LANES), lambda b, c: (b, 0, c))],
        out_shape=[jax.ShapeDtypeStruct((batch, length, LRU_WIDTH), F32),
                   jax.ShapeDtypeStruct((batch, 1, LRU_WIDTH), F32)],
        scratch_shapes=[pltpu.VMEM((length + SUBLANES, LANES), F32), pltpu.VMEM((length, LANES), F32),
                        pltpu.VMEM((length, LANES), F32)],
        compiler_params=_cparams(2),
        name="lru_prompt",
    )(proj3, proj3, conv_w, conv_b.reshape(1, -1), wr_bd, b_r.reshape(1, -1), wi_bd, b_i.reshape(1, -1),
      lam.reshape(1, -1))


def _lru_decode_kernel(xr_ref, yg_ref, cbuf_ref, h0_ref, cw_ref, cb_ref, wr_ref, br_ref, wi_ref, bi_ref, lam_ref,
                       aout_ref, hlast_ref):
    steps = xr_ref.shape[0]
    xs = [cbuf_ref[j] for j in range(CONV_W - 1)] + [xr_ref[t] for t in range(steps)]
    cw = cw_ref[...]
    h = h0_ref[...]
    n_ct = LRU_WIDTH // LANES
    for t in range(steps):
        xc = cb_ref[...]
        for j in range(CONV_W):
            xc = xc + cw[j:j + 1, :] * xs[t + j]
        parts = []
        for c in range(n_ct):
            sl = slice(c * LANES, (c + 1) * LANES)
            parts.append(_lru_gates(xc[:, sl], wr_ref[c], br_ref[:, sl], wi_ref[c], bi_ref[:, sl], lam_ref[:, sl]))
        a = jnp.concatenate([p[0] for p in parts], axis=-1)
        u = jnp.concatenate([p[1] for p in parts], axis=-1)
        h = a * h + u
        aout_ref[t] = h * _gelu_tanh(yg_ref[t])
    hlast_ref[...] = h


def _lru_decode(proj3, cbuf, h0, conv_w, conv_b, wr_bd, b_r, wi_bd, b_i, lam):
    steps, batch, _ = proj3.shape
    full2 = lambda i: (0, 0)
    full3 = lambda i: (0, 0, 0)
    return pl.pallas_call(
        _lru_decode_kernel,
        grid=(1,),
        in_specs=[pl.BlockSpec((steps, batch, LRU_WIDTH), lambda i: (0, 0, 0)),
                  pl.BlockSpec((steps, batch, LRU_WIDTH), lambda i: (0, 0, 1)),
                  pl.BlockSpec(cbuf.shape, full3), pl.BlockSpec(h0.shape, full2),
                  pl.BlockSpec(conv_w.shape, full2), pl.BlockSpec((1, LRU_WIDTH), full2),
                  pl.BlockSpec(wr_bd.shape, full3), pl.BlockSpec((1, LRU_WIDTH), full2),
                  pl.BlockSpec(wi_bd.shape, full3), pl.BlockSpec((1, LRU_WIDTH), full2),
                  pl.BlockSpec((1, LRU_WIDTH), full2)],
        out_specs=[pl.BlockSpec((steps, batch, LRU_WIDTH), full3), pl.BlockSpec((batch, LRU_WIDTH), full2)],
        out_shape=[jax.ShapeDtypeStruct((steps, batch, LRU_WIDTH), F32),
                   jax.ShapeDtypeStruct((batch, LRU_WIDTH), F32)],
        compiler_params=_cparams(1),
        name="lru_decode",
    )(proj3, proj3, cbuf, h0, conv_w, conv_b.reshape(1, -1), wr_bd, b_r.reshape(1, -1), wi_bd,
      b_i.reshape(1, -1), lam.reshape(1, -1))


def _decode_bias(slopes_hq, q_times, window, dil, steps):
    heads, nq = slopes_hq.shape
    bias = np.zeros((heads, nq, window + LANES), np.float64)
    w = np.arange(window)
    for qi, t in enumerate(q_times):
        if t < 0:
            continue
        dist = window + t - w
        ok = (dist % dil == 0) & (dist <= window)
        bias[:, qi, :window] = np.where(ok[None], -slopes_hq[:, qi, None] * dist[None], -np.inf)
        for c in range(LANES):
            d = t - c
            good = c < steps and d >= 0 and d % dil == 0
            bias[:, qi, window + c] = -slopes_hq[:, qi] * d if good else -np.inf
    return jnp.asarray(bias.astype(np.float32))


def _decode_kernel(*refs, window, steps, has_sink, has_alias):
    refs = list(refs)
    q_ref, c_ref, new_ref, bias_ref = refs[:4]
    pos = 4
    sink_ref = None
    if has_sink:
        sink_ref = refs[pos]
        pos += 1
    if has_alias:
        pos += 1
    co_ref, o_ref, lse_ref = refs[pos:pos + 3]
    c = c_ref[0]
    new = new_ref[...]
    bb, _, hb, hd, _ = c.shape
    nq = q_ref.shape[2]
    merge = lambda a: a.reshape((bb * hb,) + a.shape[2:])
    k_ext = jnp.concatenate([merge(c[:, 0]), merge(new[:, 0])], axis=-1).astype(BF16)
    v_ext = jnp.concatenate([merge(c[:, 1]), merge(new[:, 1])], axis=-1).astype(BF16)
    q = merge(q_ref[...]).astype(BF16)
    bias = merge(jnp.broadcast_to(bias_ref[...][None], (bb,) + bias_ref.shape))
    s = jnp.einsum("hqd,hdw->hqw", q, k_ext, preferred_element_type=F32) * (HEAD_DIM ** -0.5) + bias
    m = jnp.max(s, axis=-1, keepdims=True)
    e = jnp.exp(s - m)
    l = jnp.sum(e, axis=-1, keepdims=True)
    lse = m + jnp.log(l)
    p = e / l
    if has_sink:
        sink = merge(jnp.broadcast_to(sink_ref[...][None], (bb,) + sink_ref.shape))
        p = p * jax.nn.sigmoid(lse - sink[:, :, 0:1])
    o = jnp.einsum("hqw,hdw->hqd", p.astype(BF16), v_ext, preferred_element_type=F32)
    o_ref[...] = o.reshape(bb, hb, nq, hd)
    lse_ref[...] = jnp.broadcast_to(lse, (bb * hb, nq, LANES)).reshape(bb, hb, nq, LANES)
    co_ref[0] = pltpu.roll(c, window - steps, 4)
    lane = lax.broadcasted_iota(jnp.int32, new.shape, 4)
    last = pltpu.roll(c[:, :, :, :, window - LANES:], LANES - steps, 4)
    tail = pltpu.roll(new, LANES - steps, 4)
    co_ref[0, :, :, :, :, window - LANES:] = jnp.where(lane >= LANES - steps, tail, last)


def _decode_attention(layer, cache_t, prev_out, q, new_front, bias, sink, rows_per_step, heads_per_step,
                      steps, name):
    n_layers, batch, _, heads, hd, window = cache_t.shape
    nq = q.shape[2]
    bb, hb = rows_per_step, heads_per_step
    cmap = lambda b, h: (layer, b, 0, h, 0, 0)
    in_specs = [pl.BlockSpec((bb, hb, nq, hd), lambda b, h: (b, h, 0, 0)),
                pl.BlockSpec((1, bb, 2, hb, hd, window), cmap),
                pl.BlockSpec((bb, 2, hb, hd, LANES), lambda b, h: (b, 0, h, 0, 0)),
                pl.BlockSpec((hb, nq, window + LANES), lambda b, h: (h, 0, 0))]
    args = [q, cache_t, new_front, bias]
    if sink is not None:
        in_specs.append(pl.BlockSpec((hb, nq, LANES), lambda b, h: (h, 0, 0)))
        args.append(sink)
    aliases = {}
    if prev_out is not None:
        in_specs.append(pl.BlockSpec(memory_space=pl.ANY))
        aliases = {len(args): 0}
        args.append(prev_out)
    return pl.pallas_call(
        functools.partial(_decode_kernel, window=window, steps=steps, has_sink=sink is not None,
                          has_alias=prev_out is not None),
        grid=(batch // bb, heads // hb),
        in_specs=in_specs,
        out_specs=[pl.BlockSpec((1, bb, 2, hb, hd, window), cmap),
                   pl.BlockSpec((bb, hb, nq, hd), lambda b, h: (b, h, 0, 0)),
                   pl.BlockSpec((bb, hb, nq, LANES), lambda b, h: (b, h, 0, 0))],
        out_shape=[jax.ShapeDtypeStruct(cache_t.shape, F32),
                   jax.ShapeDtypeStruct((batch, heads, nq, hd), F32),
                   jax.ShapeDtypeStruct((batch, heads, nq, LANES), F32)],
        input_output_aliases=aliases,
        compiler_params=_cparams(2),
        name=name,
    )(*args)


def _router_kernel(x_ref, wt_ref, b_ref, e_ref, g_ref, c_ref):
    logits = lax.dot_general(wt_ref[...].astype(BF16), x_ref[...].astype(BF16), (((1,), (1,)), ((), ())),
                             preferred_element_type=F32)
    scores = jax.nn.sigmoid(logits)
    biased = scores + b_ref[...]
    tm = biased.shape[1]
    b3 = biased.reshape(N_EXPERT_GROUPS, GROUP_SIZE, tm)
    i3 = lax.broadcasted_iota(jnp.int32, b3.shape, 1).astype(F32)
    g1 = jnp.max(b3, axis=1, keepdims=True)
    first = jnp.min(jnp.where(b3 == g1, i3, float(GROUP_SIZE)), axis=1, keepdims=True)
    g2 = jnp.max(jnp.where(i3 == first, NEG_INF, b3), axis=1, keepdims=True)
    gs = g1 + g2
    gi = lax.broadcasted_iota(jnp.int32, gs.shape, 0).astype(F32)
    sel = jnp.zeros(gs.shape, F32)
    for _ in range(TOPK_GROUPS):
        m = jnp.max(gs, axis=0, keepdims=True)
        f = jnp.min(jnp.where(gs == m, gi, float(N_EXPERT_GROUPS)), axis=0, keepdims=True)
        hit = gi == f
        sel = jnp.where(hit, 1.0, sel)
        gs = jnp.where(hit, NEG_INF, gs)
    masked = jnp.where(sel > 0.5, b3, NEG_INF).reshape(N_EXPERTS, tm)
    ei = lax.broadcasted_iota(jnp.int32, masked.shape, 0).astype(F32)
    e_rows, g_rows = [], []
    chosen = jnp.zeros(masked.shape, F32)
    for _ in range(TOP_K):
        m = jnp.max(masked, axis=0, keepdims=True)
        f = jnp.min(jnp.where(masked == m, ei, float(N_EXPERTS)), axis=0, keepdims=True)
        hit = ei == f
        g_rows.append(jnp.sum(jnp.where(hit, scores, 0.0), axis=0, keepdims=True))
        e_rows.append(f)
        chosen = jnp.where(hit, 1.0, chosen)
        masked = jnp.where(hit, NEG_INF, masked)
    gates = jnp.concatenate(g_rows, axis=0)
    gates = gates / jnp.sum(gates, axis=0, keepdims=True) * ROUTED_SCALE
    e_ref[...] = jnp.concatenate(e_rows, axis=0).astype(jnp.int32)
    g_ref[...] = gates
    c_ref[0] = jnp.sum(chosen, axis=1, keepdims=True)


def _router(x, w_router_t, bias, tm):
    n = x.shape[0]
    nt = n // tm
    return pl.pallas_call(
        _router_kernel,
        grid=(nt,),
        in_specs=[pl.BlockSpec((tm, D_MODEL), lambda i: (i, 0)), pl.BlockSpec((N_EXPERTS, D_MODEL), lambda i: (0, 0)),
                  pl.BlockSpec((N_EXPERTS, 1), lambda i: (0, 0))],
        out_specs=[pl.BlockSpec((TOP_K, tm), lambda i: (0, i)), pl.BlockSpec((TOP_K, tm), lambda i: (0, i)),
                   pl.BlockSpec((1, N_EXPERTS, 1), lambda i: (i, 0, 0))],
        out_shape=[jax.ShapeDtypeStruct((TOP_K, n), jnp.int32), jax.ShapeDtypeStruct((TOP_K, n), F32),
                   jax.ShapeDtypeStruct((nt, N_EXPERTS, 1), F32)],
        compiler_params=_cparams(1),
        name="moe_router",
    )(x, w_router_t, bias.reshape(N_EXPERTS, 1))


KEY_EXPERT_SHIFT = 18
KEY_PAD_BIT = 17


def _dispatch_plan(top_e, gates, tile_counts, chunk, tm):
    n = top_e.shape[1]
    n_chunks = n // chunk
    pairs = TOP_K * chunk
    assert pairs < (1 << KEY_PAD_BIT) and tm <= (1 << KEY_PAD_BIT) and pairs % tm == 0
    per_chunk = lambda a: jnp.transpose(a.reshape(TOP_K, n_chunks, chunk), (1, 0, 2)).reshape(n_chunks, pairs)
    te, ga = per_chunk(top_e), per_chunk(gates)
    counts = tile_counts.reshape(n_chunks, -1, N_EXPERTS).sum(axis=1).astype(jnp.int32)
    n_pad = (-counts) % tm
    key_real = (te << KEY_EXPERT_SHIFT) + jnp.arange(pairs, dtype=jnp.int32)[None, :]
    e_pad = jnp.arange(N_EXPERTS, dtype=jnp.int32)[None, :, None]
    j_pad = jnp.arange(tm, dtype=jnp.int32)[None, None, :]
    key_pad = jnp.where(j_pad < n_pad[:, :, None],
                        (e_pad << KEY_EXPERT_SHIFT) + (1 << KEY_PAD_BIT) + j_pad,
                        (N_EXPERTS << KEY_EXPERT_SHIFT) + e_pad * tm + j_pad)
    keys = jnp.concatenate([key_real, key_pad.reshape(n_chunks, N_EXPERTS * tm)], axis=1)
    vals = jnp.concatenate([ga, jnp.zeros((n_chunks, N_EXPERTS * tm), F32)], axis=1)
    keys, vals = lax.sort((keys, vals), dimension=1, num_keys=1, is_stable=False)
    e_row = keys >> KEY_EXPERT_SHIFT
    real = ((keys >> KEY_PAD_BIT) & 1) == 0
    real = real & (e_row < N_EXPERTS)
    tok = (keys & ((1 << KEY_PAD_BIT) - 1)) % chunk
    nb = pairs // tm + N_EXPERTS
    gather_idx = jnp.where(real, tok, 0).reshape(n_chunks * nb, 1, tm)
    scatter_idx = jnp.where(real, tok, chunk).reshape(n_chunks * nb, 1, tm)
    scatter_idx = jnp.concatenate([scatter_idx, jnp.full((1, 1, tm), chunk, jnp.int32)], axis=0)
    block_e = jnp.minimum(e_row[:, ::tm], N_EXPERTS - 1).reshape(n_chunks * nb)
    used = ((counts + n_pad).sum(axis=1) // tm).astype(jnp.int32)
    return gather_idx, scatter_idx, vals.reshape(n_chunks * nb * tm, 1), block_e, used, nb


SCATTER_BATCH = 16


def _expert_kernel(be_ref, used_ref, gcur_ref, gnext_ref, scur_ref, sprev_ref, gate_ref, x_hbm, wg_ref, wu_ref,
                   wd_ref, y_hbm, x_s, y_s, xg, obuf, sem, *, tm, chunk):
    c = pl.program_id(0)
    j = pl.program_id(1)
    slot = j % 2
    used = used_ref[c]
    rows = chunk * ROW_TILES

    def gather(idx_ref, s):
        for r in range(tm):
            t = pl.multiple_of(idx_ref[0, 0, r] * ROW_TILES, ROW_TILES)
            xg[s, r * ROW_TILES:(r + 1) * ROW_TILES, :] = x_s[pl.ds(t, ROW_TILES), :]

    def scatter_add(idx_ref, s):
        for r0 in range(0, tm, SCATTER_BATCH):
            offs = [pl.multiple_of(idx_ref[0, 0, r0 + i] * ROW_TILES, ROW_TILES) for i in range(SCATTER_BATCH)]
            sums = [y_s[pl.ds(offs[i], ROW_TILES), :] + obuf[s, (r0 + i) * ROW_TILES:(r0 + i + 1) * ROW_TILES, :]
                    for i in range(SCATTER_BATCH)]
            for i in range(SCATTER_BATCH):
                y_s[pl.ds(offs[i], ROW_TILES), :] = sums[i]

    @pl.when(j == 0)
    def _():
        load = pltpu.make_async_copy(x_hbm.at[pl.ds(c * rows, rows)], x_s, sem.at[0])
        load.start()
        y_s[...] = jnp.zeros(y_s.shape, F32)

        @pl.when(c == 0)
        def _():
            obuf[...] = jnp.zeros(obuf.shape, F32)

        load.wait()
        gather(gcur_ref, slot)

    @pl.when(j < used)
    def _():
        gather(gnext_ref, 1 - slot)
        x = _load_tile_rows(xg.at[slot], tm).astype(BF16)
        hg = jnp.dot(x, wg_ref[0, 0].astype(BF16), preferred_element_type=F32)
        hu = jnp.dot(x, wu_ref[0, 0].astype(BF16), preferred_element_type=F32)
        hid = (hg * jax.nn.sigmoid(hg)) * hu
        out = jnp.dot(hid.astype(BF16), wd_ref[0, 0].astype(BF16), preferred_element_type=F32) * gate_ref[...]
        _store_tile_rows(obuf.at[slot], out, tm)
        scatter_add(sprev_ref, 1 - slot)

    @pl.when(j == used - 1)
    def _():
        scatter_add(scur_ref, slot)
        store = pltpu.make_async_copy(y_s.at[pl.ds(0, rows)], y_hbm.at[pl.ds(c * rows, rows)], sem.at[1])
        store.start()
        store.wait()


def _experts(x_tiles, plan, layer, w_gate, w_up, w_down, n_tokens, chunk, tm):
    gather_idx, scatter_idx, row_gate, block_e, used, nb = plan
    n_chunks = n_tokens // chunk
    dump_block = n_chunks * nb
    smem_blk = lambda f: pl.BlockSpec((1, 1, tm), f, memory_space=pltpu.SMEM)
    wmap = lambda c, j, be, us: (layer, be[c * nb + j], 0, 0)
    grid_spec = pltpu.PrefetchScalarGridSpec(
        num_scalar_prefetch=2,
        grid=(n_chunks, nb),
        in_specs=[smem_blk(lambda c, j, be, us: (c * nb + j, 0, 0)),
                  smem_blk(lambda c, j, be, us: (c * nb + jnp.minimum(j + 1, nb - 1), 0, 0)),
                  smem_blk(lambda c, j, be, us: (c * nb + j, 0, 0)),
                  smem_blk(lambda c, j, be, us: (jnp.where(j == 0, dump_block, c * nb + j - 1), 0, 0)),
                  pl.BlockSpec((tm, 1), lambda c, j, be, us: (c * nb + j, 0)),
                  pl.BlockSpec(memory_space=pl.ANY),
                  pl.BlockSpec((1, 1, D_MODEL, D_EXPERT), wmap), pl.BlockSpec((1, 1, D_MODEL, D_EXPERT), wmap),
                  pl.BlockSpec((1, 1, D_EXPERT, D_MODEL), wmap)],
        out_specs=pl.BlockSpec(memory_space=pl.ANY),
        scratch_shapes=[pltpu.VMEM((chunk * ROW_TILES, LANES), F32),
                        pltpu.VMEM(((chunk + 1) * ROW_TILES, LANES), F32),
                        pltpu.VMEM((2, tm * ROW_TILES, LANES), F32), pltpu.VMEM((2, tm * ROW_TILES, LANES), F32),
                        pltpu.SemaphoreType.DMA((2,))],
    )
    return pl.pallas_call(
        functools.partial(_expert_kernel, tm=tm, chunk=chunk),
        grid_spec=grid_spec,
        out_shape=jax.ShapeDtypeStruct((n_tokens * ROW_TILES, LANES), F32),
        compiler_params=_cparams(2),
        name="moe_experts",
    )(block_e, used, gather_idx, gather_idx, scatter_idx, scatter_idx, row_gate, x_tiles, w_gate, w_up, w_down)


def _combine_kernel(y_ref, x_ref, wsg_ref, wsu_ref, wsd_ref, g_ref, b_ref, o_ref, wsg_bf, wsu_bf, wsd_bf, *, tm):
    @pl.when(pl.program_id(0) == 0)
    def _():
        wsg_bf[...] = wsg_ref[...].astype(BF16)
        wsu_bf[...] = wsu_ref[...].astype(BF16)
        wsd_bf[...] = wsd_ref[...].astype(BF16)

    routed = _load_tile_rows(y_ref, tm)
    x = x_ref[...]
    xb = x.astype(BF16)
    sg = jnp.dot(xb, wsg_bf[...], preferred_element_type=F32)
    su = jnp.dot(xb, wsu_bf[...], preferred_element_type=F32)
    shared = jnp.dot(((sg * jax.nn.sigmoid(sg)) * su).astype(BF16), wsd_bf[...], preferred_element_type=F32)
    o_ref[...] = _layer_norm(DEEPNORM_ALPHA * x + (routed + shared), g_ref[...], b_ref[...])


def _combine(y_routed, x, ws_gate, ws_up, ws_down, g, b, tm):
    n = x.shape[0]
    fixed = lambda i: (0, 0)
    return pl.pallas_call(
        functools.partial(_combine_kernel, tm=tm),
        grid=(n // tm,),
        in_specs=[pl.BlockSpec((tm * ROW_TILES, LANES), lambda i: (i, 0)), pl.BlockSpec((tm, D_MODEL), lambda i: (i, 0)),
                  pl.BlockSpec(ws_gate.shape, fixed), pl.BlockSpec(ws_up.shape, fixed),
                  pl.BlockSpec(ws_down.shape, fixed),
                  pl.BlockSpec((1, D_MODEL), fixed), pl.BlockSpec((1, D_MODEL), fixed)],
        out_specs=pl.BlockSpec((tm, D_MODEL), lambda i: (i, 0)),
        out_shape=jax.ShapeDtypeStruct((n, D_MODEL), F32),
        scratch_shapes=[pltpu.VMEM(ws_gate.shape, BF16), pltpu.VMEM(ws_up.shape, BF16),
                        pltpu.VMEM(ws_down.shape, BF16)],
        compiler_params=_cparams(1),
        name="moe_combine",
    )(y_routed, x, ws_gate, ws_up, ws_down, g.reshape(1, -1), b.reshape(1, -1))


def _moe_block(x, x_tiles, layer, w_router_t, router_bias, w_gate, w_up, w_down, ws_gate, ws_up, ws_down, g, b,
               tm_route, chunk, tm_expert, tm_combine):
    n = x.shape[0]
    top_e, gates, tile_counts = _router(x, w_router_t, router_bias, tm_route)
    plan = _dispatch_plan(top_e, gates, tile_counts, chunk, tm_expert)
    y_routed = _experts(x_tiles, plan, layer, w_gate, w_up, w_down, n, chunk, tm_expert)
    return _combine(y_routed, x, ws_gate, ws_up, ws_down, g, b, tm_combine)


def _to_cache_t(cache):
    return jnp.transpose(cache, (0, 1, 3, 4, 5, 2))


def _from_cache_t(cache_t):
    return jnp.transpose(cache_t, (0, 1, 5, 2, 3, 4))


def _new_kv_columns(k, v, steps, batch, heads):
    kv = jnp.stack([k, v]).reshape(2, steps, batch, heads, HEAD_DIM)
    kv = jnp.transpose(kv, (2, 0, 3, 4, 1))
    return jnp.pad(kv, ((0, 0),) * 4 + ((0, LANES - steps),))


def kernel(x_prompt, x_sample, state_lru_h, state_lru_conv, cache_swa_kv, cache_dil1_kv, cache_dil2_kv,
           cache_dil3_kv, w_in_even, conv_w, conv_b, w_rgate, b_rgate, w_igate, b_igate, lru_lambda, swa_sinks,
           w_out_even, w_in_odd, w_out_odd, ln1_g, ln1_b, ln2_g, ln2_b, w_router, router_bias, w_exp_gate,
           w_exp_up, w_exp_down, w_sh_gate, w_sh_up, w_sh_down):
    batch, seq, _ = x_prompt.shape
    dbatch, steps, _ = x_sample.shape
    n_p = batch * seq
    n_s = dbatch * steps
    xp = x_prompt.reshape(n_p, D_MODEL)
    xs = jnp.transpose(x_sample, (1, 0, 2)).reshape(n_s, D_MODEL)

    slopes8 = _alibi_slopes(SWA_HEADS)
    pad_q = SUBLANES - steps
    q_times_dil = list(range(steps)) + [-1] * pad_q
    swa_q_times = [t for t in range(steps) for _ in range(SWA_GROUP)]
    swa_slopes = np.stack([np.tile(slopes8[kv * SWA_GROUP:(kv + 1) * SWA_GROUP], steps)
                           for kv in range(SWA_KV_HEADS)])
    swa_bias = _decode_bias(swa_slopes, swa_q_times, SWA_WINDOW, 1, steps)
    dil_bias = [_decode_bias(np.tile(_alibi_slopes(DIL_HEADS)[:, None], (1, SUBLANES)), q_times_dil,
                             DIL_WINDOWS[g], DIL_RATES[g], steps) for g in range(N_DIL)]

    et_flat = _head_expand_matrix(DIL_HEADS, DIL_HEADS)
    et_fold = _head_expand_matrix(DIL_HEADS, FOLD_HEADS)
    ets_prompt = [et_flat if d == 1 else et_fold for d in DIL_RATES]
    ets_sample = [et_flat] * N_DIL

    swa_t = _to_cache_t(cache_swa_kv)
    dil_t = [_to_cache_t(c) for c in (cache_dil1_kv, cache_dil2_kv, cache_dil3_kv)]
    swa_out = None
    dil_out = [None] * N_DIL
    lru_h_p, lru_h_s, lru_c_p, lru_c_s, swa_p = [], [], [], [], []
    dil_p = [[] for _ in range(N_DIL)]

    for l in range(DEPTH):
        j = l // 2
        if l % 2 == 0:
            wr_bd, wi_bd = _gate_blockdiag(w_rgate[j]), _gate_blockdiag(w_igate[j])
            proj = _matmul(xp, w_in_even[j], 1024, 896, "inproj_even_prompt")
            proj3 = proj.reshape(batch, seq, -1)
            a_out, h_last = _lru_prompt(proj3, conv_w[j], conv_b[j], wr_bd, b_rgate[j], wi_bd, b_igate[j],
                                        lru_lambda[j])
            (o_swa,) = _banded_attention(proj3, batch, seq, 1, (0, 2), (0, 12), (0, 13), SWA_KV_HEADS * HEAD_DIM,
                                         SWA_HEADS, SWA_GROUP, slopes8, 1, swa_sinks[j], "swa_prompt")
            lru_h_p.append(h_last[:, 0])
            lru_c_p.append(proj3[:, seq - (CONV_W - 1):, :LRU_WIDTH])
            keep = min(SWA_WINDOW, seq)
            swa_p.append(proj3[:, seq - keep:, 2 * LRU_WIDTH + SWA_HEADS * HEAD_DIM:]
                         .reshape(batch, keep, 2, SWA_KV_HEADS, HEAD_DIM))
            xp, xp_tiles = _outproj_even(xp, a_out.reshape(n_p, -1), o_swa.reshape(n_p, -1), w_out_even[j],
                                         ln1_g[l], ln1_b[l], 512)
            proj_s = _matmul(xs, w_in_even[j], n_s, 896, "inproj_even_sample")
            proj_s3 = proj_s.reshape(steps, dbatch, -1)
            cbuf = jnp.transpose(state_lru_conv[j], (1, 0, 2))
            a_s, h_s = _lru_decode(proj_s3, cbuf, state_lru_h[j], conv_w[j], conv_b[j], wr_bd, b_rgate[j], wi_bd,
                                   b_igate[j], lru_lambda[j])
            lru_h_s.append(h_s)
            xr_s = proj_s3[:, :, :LRU_WIDTH]
            lru_c_s.append(jnp.transpose(jnp.concatenate([cbuf, xr_s], axis=0)[-(CONV_W - 1):], (1, 0, 2)))
            o0 = 2 * LRU_WIDTH
            q_s = proj_s[:, o0:o0 + SWA_HEADS * HEAD_DIM].reshape(steps, dbatch, SWA_KV_HEADS, SWA_GROUP, HEAD_DIM)
            q_s = jnp.transpose(q_s, (1, 2, 0, 3, 4)).reshape(dbatch, SWA_KV_HEADS, steps * SWA_GROUP, HEAD_DIM)
            o1 = o0 + SWA_HEADS * HEAD_DIM
            o2 = o1 + SWA_KV_HEADS * HEAD_DIM
            front = _new_kv_columns(proj_s[:, o1:o2], proj_s[:, o2:], steps, dbatch, SWA_KV_HEADS)
            sink = jnp.broadcast_to(
                jnp.tile(swa_sinks[j].reshape(SWA_KV_HEADS, SWA_GROUP), (1, steps))[:, :, None],
                (SWA_KV_HEADS, steps * SWA_GROUP, LANES))
            swa_out, o_dec, _ = _decode_attention(j, swa_t, swa_out, q_s, front, swa_bias, sink,
                                                  min(dbatch, 16), SWA_KV_HEADS, steps, "swa_decode")
            o_dec = o_dec.reshape(dbatch, SWA_KV_HEADS, steps, SWA_GROUP, HEAD_DIM)
            o_dec = jnp.transpose(o_dec, (2, 0, 1, 3, 4)).reshape(n_s, SWA_HEADS * HEAD_DIM)
            xs, xs_tiles = _outproj_even(xs, a_s.reshape(n_s, -1), o_dec, w_out_even[j], ln1_g[l], ln1_b[l], n_s)
        else:
            proj = _matmul(xp, w_in_odd[j], 1024, 768, "inproj_odd_prompt")
            row_w = proj.shape[1]
            proj3 = proj.reshape(batch, seq, row_w)
            os_, ls_ = [], []
            for g in range(N_DIL):
                dil = DIL_RATES[g]
                if dil == 1:
                    n_blk = row_w // (DIL_HEADS * HEAD_DIM)
                    o_g, lse_g = _banded_attention(proj3, batch, seq, 1, (n_blk, 3 * g), (n_blk, 3 * g + 1),
                                                   (n_blk, 3 * g + 2), DIL_HEADS * HEAD_DIM, DIL_HEADS, 1,
                                                   _alibi_slopes(DIL_HEADS), dil, None, "dil_prompt")
                else:
                    o_g, lse_g = _fold_attention(proj3, g, dil, _alibi_slopes(DIL_HEADS))
                os_.append(o_g.reshape(n_p, -1))
                ls_.append(lse_g.reshape(n_p, -1))
                keep = min(DIL_WINDOWS[g], seq)
                c0 = g * 3 * DIL_HEADS * HEAD_DIM + DIL_HEADS * HEAD_DIM
                dil_p[g].append(proj3[:, seq - keep:, c0:c0 + 2 * DIL_HEADS * HEAD_DIM]
                                .reshape(batch, keep, 2, DIL_HEADS, HEAD_DIM))
            xp, xp_tiles = _outproj_odd(xp, os_, ls_, ets_prompt, w_out_odd[j], ln1_g[l], ln1_b[l], 512)
            proj_s = _matmul(xs, w_in_odd[j], n_s, 768, "inproj_odd_sample")
            os_, ls_ = [], []
            gw = DIL_HEADS * HEAD_DIM
            for g in range(N_DIL):
                c0 = g * 3 * gw
                q_s = proj_s[:, c0:c0 + gw].reshape(steps, dbatch, DIL_HEADS, HEAD_DIM)
                q_s = jnp.pad(jnp.transpose(q_s, (1, 2, 0, 3)), ((0, 0), (0, 0), (0, pad_q), (0, 0)))
                front = _new_kv_columns(proj_s[:, c0 + gw:c0 + 2 * gw], proj_s[:, c0 + 2 * gw:c0 + 3 * gw],
                                        steps, dbatch, DIL_HEADS)
                rows_per_step, heads_per_step = ((8, 8), (2, 8), (1, 4))[g]
                dil_out[g], o_dec, lse_dec = _decode_attention(j, dil_t[g], dil_out[g], q_s, front,
                                                               dil_bias[g], None, min(dbatch, rows_per_step),
                                                               heads_per_step, steps, "dil_decode")
                o_dec = jnp.transpose(o_dec[:, :, :steps], (2, 0, 1, 3)).reshape(n_s, gw)
                lse_dec = jnp.transpose(lse_dec[:, :, :steps, 0], (2, 0, 1)).reshape(n_s, DIL_HEADS)
                os_.append(o_dec)
                ls_.append(jnp.pad(lse_dec, ((0, 0), (0, LANES - DIL_HEADS))))
            xs, xs_tiles = _outproj_odd(xs, os_, ls_, ets_sample, w_out_odd[j], ln1_g[l], ln1_b[l], n_s)

        moe_w = (l, jnp.transpose(w_router[l]), router_bias[l], w_exp_gate, w_exp_up, w_exp_down,
                 w_sh_gate[l], w_sh_up[l], w_sh_down[l], ln2_g[l], ln2_b[l])
        xp = _moe_block(xp, xp_tiles, *moe_w, tm_route=512, chunk=min(n_p, 4096), tm_expert=256, tm_combine=512)
        xs = _moe_block(xs, xs_tiles, *moe_w, tm_route=n_s, chunk=n_s, tm_expert=128, tm_combine=n_s)

    y_prompt = xp.reshape(batch, seq, D_MODEL)
    y_sample = jnp.transpose(xs.reshape(steps, dbatch, D_MODEL), (1, 0, 2))
    return (y_prompt, y_sample, jnp.stack(lru_h_p), jnp.stack(lru_h_s), jnp.stack(lru_c_p), jnp.stack(lru_c_s),
            jnp.stack(swa_p), _from_cache_t(swa_out),
            jnp.stack(dil_p[0]), _from_cache_t(dil_out[0]), jnp.stack(dil_p[1]), _from_cache_t(dil_out[1]),
            jnp.stack(dil_p[2]), _from_cache_t(dil_out[2]))
```

```python
import functools

import numpy as np
import jax
import jax.numpy as jnp
from jax import lax
from jax.experimental import pallas as pl
from jax.experimental.pallas import tpu as pltpu

F32 = jnp.float32
BF16 = jnp.bfloat16

D_MODEL = 1024
HEAD_DIM = 64
LRU_WIDTH = 512
LRU_C = 8.0
CONV_W = 4
SWA_HEADS = 8
SWA_KV_HEADS = 2
SWA_GROUP = SWA_HEADS // SWA_KV_HEADS
SWA_WINDOW = 128
DIL_WINDOWS = (128, 512, 2048)
DIL_RATES = (1, 4, 16)
N_DIL = 3
DIL_HEADS = 8
BAND = 128
N_EXPERTS = 64
N_EXPERT_GROUPS = 8
GROUP_SIZE = N_EXPERTS // N_EXPERT_GROUPS
TOPK_GROUPS = 4
TOP_K = 8
D_EXPERT = 256
ROUTED_SCALE = 2.5
LN_EPS = 1e-5
DEPTH = 4
DEEPNORM_ALPHA = (2 * DEPTH) ** 0.25

LANES = 128
SUBLANES = 8
ROW_TILES = D_MODEL // LANES
VMEM_LIMIT = 48 * 1024 * 1024

NEG_INF = float("-inf")


def _cparams(n_grid_dims):
    return pltpu.CompilerParams(dimension_semantics=("arbitrary",) * n_grid_dims, vmem_limit_bytes=VMEM_LIMIT)


def _alibi_slopes(n):
    return np.exp2(-8.0 * np.arange(1, n + 1, dtype=np.float64) / n)


def _bdot(a, b):
    return jnp.dot(a.astype(BF16), b.astype(BF16), preferred_element_type=F32)


def _layer_norm(z, g, b):
    zc = z - jnp.mean(z, -1, keepdims=True)
    var = jnp.mean(zc * zc, -1, keepdims=True)
    return zc * lax.rsqrt(var + LN_EPS) * g + b


def _store_tile_rows(ref, val, rows):
    for c in range(ROW_TILES):
        ref[pl.ds(c, rows, stride=ROW_TILES), :] = val[:, c * LANES:(c + 1) * LANES]


def _load_tile_rows(ref, rows):
    return jnp.concatenate([ref[pl.ds(c, rows, stride=ROW_TILES), :] for c in range(ROW_TILES)], axis=-1)


def _mm_kernel(x_ref, w_ref, o_ref):
    o_ref[...] = _bdot(x_ref[...], w_ref[...])


def _matmul(x, w, tm, tn, name):
    m, k = x.shape
    n = w.shape[1]
    return pl.pallas_call(
        _mm_kernel,
        grid=(m // tm, n // tn),
        in_specs=[pl.BlockSpec((tm, k), lambda i, j: (i, 0)), pl.BlockSpec((k, tn), lambda i, j: (0, j))],
        out_specs=pl.BlockSpec((tm, tn), lambda i, j: (i, j)),
        out_shape=jax.ShapeDtypeStruct((m, n), F32),
        compiler_params=_cparams(2),
        name=name,
    )(x, w)


def _outproj_even_kernel(x_ref, a_ref, o_ref, w_ref, g_ref, b_ref, y_ref, yt_ref, wbf_ref):
    @pl.when(pl.program_id(0) == 0)
    def _():
        wbf_ref[...] = w_ref[...].astype(BF16)

    half = a_ref.shape[1]
    f = jnp.dot(a_ref[...].astype(BF16), wbf_ref[0:half, :], preferred_element_type=F32)
    f = f + jnp.dot(o_ref[...].astype(BF16), wbf_ref[half:, :], preferred_element_type=F32)
    y = _layer_norm(DEEPNORM_ALPHA * x_ref[...] + f, g_ref[...], b_ref[...])
    y_ref[...] = y
    _store_tile_rows(yt_ref, y, y.shape[0])


def _outproj_even(x, a, o, w, g, b, tm):
    m = x.shape[0]
    row = lambda i: (i, 0)
    fixed = lambda i: (0, 0)
    return pl.pallas_call(
        _outproj_even_kernel,
        grid=(m // tm,),
        in_specs=[pl.BlockSpec((tm, D_MODEL), row), pl.BlockSpec((tm, a.shape[1]), row),
                  pl.BlockSpec((tm, o.shape[1]), row), pl.BlockSpec(w.shape, fixed),
                  pl.BlockSpec((1, D_MODEL), fixed), pl.BlockSpec((1, D_MODEL), fixed)],
        out_specs=[pl.BlockSpec((tm, D_MODEL), row), pl.BlockSpec((tm * ROW_TILES, LANES), row)],
        out_shape=[jax.ShapeDtypeStruct((m, D_MODEL), F32), jax.ShapeDtypeStruct((m * ROW_TILES, LANES), F32)],
        scratch_shapes=[pltpu.VMEM(w.shape, BF16)],
        compiler_params=_cparams(1),
        name="outproj_even",
    )(x, a, o, w, g.reshape(1, -1), b.reshape(1, -1))


def _outproj_odd_kernel(x_ref, o1_ref, o2_ref, o3_ref, l1_ref, l2_ref, l3_ref, e1_ref, e2_ref, e3_ref, w_ref, g_ref,
                        b_ref, y_ref, yt_ref, wbf_ref):
    @pl.when(pl.program_id(0) == 0)
    def _():
        wbf_ref[...] = w_ref[...].astype(BF16)

    def expand(l_ref, e_ref):
        lv = l_ref[...]
        hi = lv.astype(BF16)
        r1 = lv - hi.astype(F32)
        mid = r1.astype(BF16)
        lo = (r1 - mid.astype(F32)).astype(BF16)
        e = e_ref[...]
        dot = lambda a: jnp.dot(a, e, preferred_element_type=F32)
        return (dot(hi) + dot(mid)) + dot(lo)

    l1, l2, l3 = expand(l1_ref, e1_ref), expand(l2_ref, e2_ref), expand(l3_ref, e3_ref)
    m = jnp.maximum(jnp.maximum(l1, l2), l3)
    w1, w2, w3 = jnp.exp(l1 - m), jnp.exp(l2 - m), jnp.exp(l3 - m)
    mix = (w1 * o1_ref[...] + w2 * o2_ref[...] + w3 * o3_ref[...]) / (w1 + w2 + w3)
    f = jnp.dot(mix.astype(BF16), wbf_ref[...], preferred_element_type=F32)
    y = _layer_norm(DEEPNORM_ALPHA * x_ref[...] + f, g_ref[...], b_ref[...])
    y_ref[...] = y
    _store_tile_rows(yt_ref, y, y.shape[0])


def _head_expand_matrix(n_heads, heads_per_tile):
    n_tiles = n_heads // heads_per_tile
    et = np.zeros((n_tiles * LANES, n_heads * HEAD_DIM), np.float32)
    for h in range(n_heads):
        et[(h // heads_per_tile) * LANES + h % heads_per_tile, h * HEAD_DIM:(h + 1) * HEAD_DIM] = 1.0
    return jnp.asarray(et, dtype=BF16)


def _outproj_odd(x, os_, ls_, ets, w, g, b, tm):
    m = x.shape[0]
    row = lambda i: (i, 0)
    fixed = lambda i: (0, 0)
    width = os_[0].shape[1]
    return pl.pallas_call(
        _outproj_odd_kernel,
        grid=(m // tm,),
        in_specs=[pl.BlockSpec((tm, D_MODEL), row)] + [pl.BlockSpec((tm, width), row)] * 3
                 + [pl.BlockSpec((tm, l.shape[1]), row) for l in ls_]
                 + [pl.BlockSpec(e.shape, fixed) for e in ets]
                 + [pl.BlockSpec(w.shape, fixed), pl.BlockSpec((1, D_MODEL), fixed), pl.BlockSpec((1, D_MODEL), fixed)],
        out_specs=[pl.BlockSpec((tm, D_MODEL), row), pl.BlockSpec((tm * ROW_TILES, LANES), row)],
        out_shape=[jax.ShapeDtypeStruct((m, D_MODEL), F32), jax.ShapeDtypeStruct((m * ROW_TILES, LANES), F32)],
        scratch_shapes=[pltpu.VMEM(w.shape, BF16)],
        compiler_params=_cparams(1),
        name="outproj_odd",
    )(x, *os_, *ls_, *ets, w, g.reshape(1, -1), b.reshape(1, -1))


def _band_bias(slopes, dist_scale, window):
    qi = np.arange(BAND)[:, None]
    kj = np.arange(2 * BAND)[None, :]
    dist = qi + BAND - kj
    valid = (dist >= 0) & (dist <= window)
    bias = -slopes[:, None, None] * (dist_scale * dist)[None].astype(np.float64)
    return jnp.asarray(np.where(valid[None], bias, -np.inf).astype(np.float32))


def _band_kernel(*refs, n_heads, group, has_sink):
    if has_sink:
        q_ref, kp_ref, kc_ref, vp_ref, vc_ref, bias_ref, sink_ref, o_ref = refs
    else:
        q_ref, kp_ref, kc_ref, vp_ref, vc_ref, bias_ref, o_ref, lse_ref = refs
    j = pl.program_id(2)
    q = q_ref[0]
    k = jnp.concatenate([kp_ref[0], kc_ref[0]], axis=0)
    v = jnp.concatenate([vp_ref[0], vc_ref[0]], axis=0)
    col = lax.broadcasted_iota(jnp.int32, (BAND, 2 * BAND), 1)
    keep = (col >= BAND) | (j > 0)
    lane = lax.broadcasted_iota(jnp.int32, (BAND, LANES), 1)
    lse_all = jnp.zeros((BAND, LANES), F32)
    for h in range(n_heads):
        kh = h // group
        qh = q[:, h * HEAD_DIM:(h + 1) * HEAD_DIM].astype(BF16)
        kk = k[:, kh * HEAD_DIM:(kh + 1) * HEAD_DIM].astype(BF16)
        vv = v[:, kh * HEAD_DIM:(kh + 1) * HEAD_DIM].astype(BF16)
        s = lax.dot_general(qh, kk, (((1,), (1,)), ((), ())), preferred_element_type=F32)
        s = s * (HEAD_DIM ** -0.5) + bias_ref[h]
        s = jnp.where(keep, s, NEG_INF)
        m = jnp.max(s, axis=-1, keepdims=True)
        e = jnp.exp(s - m)
        l = jnp.sum(e, axis=-1, keepdims=True)
        o = jnp.dot(e.astype(BF16), vv, preferred_element_type=F32) / l
        lse = m + jnp.log(l)
        if has_sink:
            o = o * jax.nn.sigmoid(lse - sink_ref[h])
        else:
            lse_all = jnp.where(lane == h, lse, lse_all)
        o_ref[0, :, h * HEAD_DIM:(h + 1) * HEAD_DIM] = o
    if not has_sink:
        lse_ref[0] = lse_all


def _banded_attention(src, batch, lf, fold, qcol, kcol, vcol, kv_width, n_heads, group, slopes, dist_scale,
                      sinks, name):
    nb = lf // BAND
    qw = n_heads * HEAD_DIM
    bias = _band_bias(slopes, dist_scale, BAND)
    qmap = lambda b, r, j: (b, j, r * qcol[0] + qcol[1])
    kprev = lambda b, r, j: (b, jnp.maximum(j - 1, 0), r * kcol[0] + kcol[1])
    kcur = lambda b, r, j: (b, j, r * kcol[0] + kcol[1])
    vprev = lambda b, r, j: (b, jnp.maximum(j - 1, 0), r * vcol[0] + vcol[1])
    vcur = lambda b, r, j: (b, j, r * vcol[0] + vcol[1])
    in_specs = [pl.BlockSpec((1, BAND, qw), qmap),
                pl.BlockSpec((1, BAND, kv_width), kprev), pl.BlockSpec((1, BAND, kv_width), kcur),
                pl.BlockSpec((1, BAND, kv_width), vprev), pl.BlockSpec((1, BAND, kv_width), vcur),
                pl.BlockSpec(bias.shape, lambda b, r, j: (0, 0, 0))]
    args = [src, src, src, src, src, bias]
    omap = lambda b, r, j: (b, j, r)
    out_specs = [pl.BlockSpec((1, BAND, qw), omap)]
    out_shape = [jax.ShapeDtypeStruct((batch, lf, fold * qw), F32)]
    if sinks is not None:
        in_specs.append(pl.BlockSpec(memory_space=pltpu.SMEM))
        args.append(sinks)
    else:
        out_specs.append(pl.BlockSpec((1, BAND, LANES), omap))
        out_shape.append(jax.ShapeDtypeStruct((batch, lf, fold * LANES), F32))
    return pl.pallas_call(
        functools.partial(_band_kernel, n_heads=n_heads, group=group, has_sink=sinks is not None),
        grid=(batch, fold, nb),
        in_specs=in_specs, out_specs=out_specs, out_shape=out_shape,
        compiler_params=_cparams(3),
        name=name,
    )(*args)


FOLD_HEADS = LANES // HEAD_DIM


def _fold_band_kernel(*refs, dil, has_prev):
    if has_prev:
        q_ref, kp_ref, kc_ref, vp_ref, vc_ref, bias_ref, o_ref, lse_ref, q_s, k_s, v_s, o_s, lse_s = refs
    else:
        q_ref, kc_ref, vc_ref, bias_ref, o_ref, lse_ref, q_s, k_s, v_s, o_s, lse_s = refs
    j = pl.program_id(1)
    hh = pl.program_id(2)
    kr = 2 * BAND if has_prev else BAND
    for rho in range(dil):
        q_s[rho * BAND:(rho + 1) * BAND, :] = q_ref[0, pl.ds(rho, BAND, stride=dil), :]
        if has_prev:
            k_s[rho * kr:rho * kr + BAND, :] = kp_ref[0, pl.ds(rho, BAND, stride=dil), :]
            v_s[rho * kr:rho * kr + BAND, :] = vp_ref[0, pl.ds(rho, BAND, stride=dil), :]
        k_s[rho * kr + kr - BAND:(rho + 1) * kr, :] = kc_ref[0, pl.ds(rho, BAND, stride=dil), :]
        v_s[rho * kr + kr - BAND:(rho + 1) * kr, :] = vc_ref[0, pl.ds(rho, BAND, stride=dil), :]
    col = lax.broadcasted_iota(jnp.int32, (BAND, kr), 1)
    keep = (col >= kr - BAND) | (j > 0)
    lane = lax.broadcasted_iota(jnp.int32, (BAND, LANES), 1)

    def body(rho, carry):
        q = q_s[pl.ds(pl.multiple_of(rho * BAND, BAND), BAND), :]
        k = k_s[pl.ds(pl.multiple_of(rho * kr, BAND), kr), :]
        v = v_s[pl.ds(pl.multiple_of(rho * kr, BAND), kr), :]
        outs = []
        lse_all = jnp.zeros((BAND, LANES), F32)
        for h in range(FOLD_HEADS):
            sl = slice(h * HEAD_DIM, (h + 1) * HEAD_DIM)
            s = lax.dot_general(q[:, sl].astype(BF16), k[:, sl].astype(BF16), (((1,), (1,)), ((), ())),
                                preferred_element_type=F32)
            s = s * (HEAD_DIM ** -0.5) + bias_ref[hh * FOLD_HEADS + h]
            if has_prev:
                s = jnp.where(keep, s, NEG_INF)
            m = jnp.max(s, axis=-1, keepdims=True)
            e = jnp.exp(s - m)
            l = jnp.sum(e, axis=-1, keepdims=True)
            outs.append(jnp.dot(e.astype(BF16), v[:, sl].astype(BF16), preferred_element_type=F32) / l)
            lse_all = jnp.where(lane == h, m + jnp.log(l), lse_all)
        o_s[pl.ds(pl.multiple_of(rho * BAND, BAND), BAND), :] = jnp.concatenate(outs, axis=-1)
        lse_s[pl.ds(pl.multiple_of(rho * BAND, BAND), BAND), :] = lse_all
        return carry

    lax.fori_loop(0, dil, body, 0, unroll=2)
    for rho in range(dil):
        o_ref[0, pl.ds(rho, BAND, stride=dil), :] = o_s[rho * BAND:(rho + 1) * BAND, :]
        lse_ref[0, pl.ds(rho, BAND, stride=dil), :] = lse_s[rho * BAND:(rho + 1) * BAND, :]


def _fold_attention(proj3, group, dil, slopes):
    batch, seq, _ = proj3.shape
    rows = BAND * dil
    nch = seq // rows
    has_prev = nch > 1
    width = FOLD_HEADS * HEAD_DIM
    n_hh = DIL_HEADS // FOLD_HEADS
    base = group * 3 * n_hh
    bias = _band_bias(slopes, dil, BAND)
    if not has_prev:
        bias = bias[:, :, BAND:]
    blk = (1, rows, width)
    cur = lambda part: (lambda b, j, h: (b, j, base + part * n_hh + h))
    prev = lambda part: (lambda b, j, h: (b, jnp.maximum(j - 1, 0), base + part * n_hh + h))
    in_specs = [pl.BlockSpec(blk, cur(0))]
    args = [proj3]
    for part in (1, 2):
        if has_prev:
            in_specs.append(pl.BlockSpec(blk, prev(part)))
            args.append(proj3)
        in_specs.append(pl.BlockSpec(blk, cur(part)))
        args.append(proj3)
    in_specs.append(pl.BlockSpec(bias.shape, lambda b, j, h: (0, 0, 0)))
    args.append(bias)
    kr = 2 * BAND if has_prev else BAND
    omap = lambda b, j, h: (b, j, h)
    return pl.pallas_call(
        functools.partial(_fold_band_kernel, dil=dil, has_prev=has_prev),
        grid=(batch, nch, n_hh),
        in_specs=in_specs,
        out_specs=[pl.BlockSpec(blk, omap), pl.BlockSpec((1, rows, LANES), omap)],
        out_shape=[jax.ShapeDtypeStruct((batch, seq, DIL_HEADS * HEAD_DIM), F32),
                   jax.ShapeDtypeStruct((batch, seq, n_hh * LANES), F32)],
        scratch_shapes=[pltpu.VMEM((rows, width), F32), pltpu.VMEM((dil * kr, width), F32),
                        pltpu.VMEM((dil * kr, width), F32), pltpu.VMEM((rows, width), F32),
                        pltpu.VMEM((rows, LANES), F32)],
        compiler_params=_cparams(3),
        name="dil_prompt_fold",
    )(*args)


def _gelu_tanh(x):
    return 0.5 * x * (1.0 + jnp.tanh(0.7978845608028654 * (x + 0.044715 * x * x * x)))


def _softplus(z):
    return jnp.maximum(z, 0.0) + jnp.log1p(jnp.exp(-jnp.abs(z)))


def _lru_gates(xc, wr, br, wi, bi, lam):
    r = jax.nn.sigmoid(_bdot(xc, wr) + br)
    i = jax.nn.sigmoid(_bdot(xc, wi) + bi)
    log_a = -LRU_C * r * _softplus(-lam)
    a = jnp.exp(log_a)
    th = jnp.tanh(log_a)
    u = jnp.sqrt(-2.0 * th / (1.0 - th)) * (i * xc)
    return a, u


def _lru_prompt_kernel(xr_ref, yg_ref, cw_ref, cb_ref, wr_ref, br_ref, wi_ref, bi_ref, lam_ref,
                       aout_ref, hlast_ref, xpad_ref, a_ref, u_ref):
    length = xr_ref.shape[1]
    xr = xr_ref[0]
    xpad_ref[0:SUBLANES, :] = jnp.zeros((SUBLANES, LANES), F32)
    xpad_ref[SUBLANES:, :] = xr
    cw = cw_ref[...]
    xc = cb_ref[...] + cw[CONV_W - 1:CONV_W, :] * xr
    for j in range(CONV_W - 1):
        shift = CONV_W - 1 - j
        xc = xc + cw[j:j + 1, :] * xpad_ref[pl.ds(SUBLANES - shift, length), :]
    a, u = _lru_gates(xc, wr_ref[0], br_ref[...], wi_ref[0], bi_ref[...], lam_ref[...])
    a_ref[...] = a
    u_ref[...] = u
    row = lax.broadcasted_iota(jnp.int32, (SUBLANES, LANES), 0)

    def body(blk, carry):
        off = pl.multiple_of(blk * SUBLANES, SUBLANES)
        ab = a_ref[pl.ds(off, SUBLANES), :]
        ub = u_ref[pl.ds(off, SUBLANES), :]
        for d in (1, 2, 4):
            ush = jnp.where(row >= d, pltpu.roll(ub, d, 0), 0.0)
            ash = jnp.where(row >= d, pltpu.roll(ab, d, 0), 1.0)
            ub = ab * ush + ub
            ab = ab * ash
        h = ab * carry + ub
        u_ref[pl.ds(off, SUBLANES), :] = h
        return jnp.broadcast_to(h[SUBLANES - 1:SUBLANES, :], (SUBLANES, LANES))

    carry = lax.fori_loop(0, length // SUBLANES, body, jnp.zeros((SUBLANES, LANES), F32), unroll=4)
    hlast_ref[0] = carry[0:1, :]
    aout_ref[0] = u_ref[...] * _gelu_tanh(yg_ref[0])


def _gate_blockdiag(w):
    z = jnp.zeros((4, LANES, LANES), F32)
    z = z.at[:, 0:HEAD_DIM, 0:HEAD_DIM].set(w[0::2])
    z = z.at[:, HEAD_DIM:, HEAD_DIM:].set(w[1::2])
    return z


def _lru_prompt(proj3, conv_w, conv_b, wr_bd, b_r, wi_bd, b_i, lam):
    batch, length, _ = proj3.shape
    n_ct = LRU_WIDTH // LANES
    vec = lambda b, c: (0, c)
    return pl.pallas_call(
        _lru_prompt_kernel,
        grid=(batch, n_ct),
        in_specs=[pl.BlockSpec((1, length, LANES), lambda b, c: (b, 0, c)),
                  pl.BlockSpec((1, length, LANES), lambda b, c: (b, 0, n_ct + c)),
                  pl.BlockSpec((CONV_W, LANES), vec), pl.BlockSpec((1, LANES), vec),
                  pl.BlockSpec((1, LANES, LANES), lambda b, c: (c, 0, 0)), pl.BlockSpec((1, LANES), vec),
                  pl.BlockSpec((1, LANES, LANES), lambda b, c: (c, 0, 0)), pl.BlockSpec((1, LANES), vec),
                  pl.BlockSpec((1, LANES), vec)],
        out_specs=[pl.BlockSpec((1, length, LANES), lambda b, c: (b, 0, c)),
                   pl.BlockSpec((1, 1, LANES), lambda b, c: (b, 0, c))],
        out_shape=[jax.ShapeDtypeStruct((batch, length, LRU_WIDTH), F32),
                   jax.ShapeDtypeStruct((batch, 1, LRU_WIDTH), F32)],
        scratch_shapes=[pltpu.VMEM((length + SUBLANES, LANES), F32), pltpu.VMEM((length, LANES), F32),
                        pltpu.VMEM((length, LANES), F32)],
        compiler_params=_cparams(2),
        name="lru_prompt",
    )(proj3, proj3, conv_w, conv_b.reshape(1, -1), wr_bd, b_r.reshape(1, -1), wi_bd, b_i.reshape(1, -1),
      lam.reshape(1, -1))


def _lru_decode_kernel(xr_ref, yg_ref, cbuf_ref, h0_ref, cw_ref, cb_ref, wr_ref, br_ref, wi_ref, bi_ref, lam_ref,
                       aout_ref, hlast_ref):
    steps = xr_ref.shape[0]
    xs = [cbuf_ref[j] for j in range(CONV_W - 1)] + [xr_ref[t] for t in range(steps)]
    cw = cw_ref[...]
    h = h0_ref[...]
    n_ct = LRU_WIDTH // LANES
    for t in range(steps):
        xc = cb_ref[...]
        for j in range(CONV_W):
            xc = xc + cw[j:j + 1, :] * xs[t + j]
        parts = []
        for c in range(n_ct):
            sl = slice(c * LANES, (c + 1) * LANES)
            parts.append(_lru_gates(xc[:, sl], wr_ref[c], br_ref[:, sl], wi_ref[c], bi_ref[:, sl], lam_ref[:, sl]))
        a = jnp.concatenate([p[0] for p in parts], axis=-1)
        u = jnp.concatenate([p[1] for p in parts], axis=-1)
        h = a * h + u
        aout_ref[t] = h * _gelu_tanh(yg_ref[t])
    hlast_ref[...] = h


def _lru_decode(proj3, cbuf, h0, conv_w, conv_b, wr_bd, b_r, wi_bd, b_i, lam):
    steps, batch, _ = proj3.shape
    full2 = lambda i: (0, 0)
    full3 = lambda i: (0, 0, 0)
    return pl.pallas_call(
        _lru_decode_kernel,
        grid=(1,),
        in_specs=[pl.BlockSpec((steps, batch, LRU_WIDTH), lambda i: (0, 0, 0)),
                  pl.BlockSpec((steps, batch, LRU_WIDTH), lambda i: (0, 0, 1)),
                  pl.BlockSpec(cbuf.shape, full3), pl.BlockSpec(h0.shape, full2),
                  pl.BlockSpec(conv_w.shape, full2), pl.BlockSpec((1, LRU_WIDTH), full2),
                  pl.BlockSpec(wr_bd.shape, full3), pl.BlockSpec((1, LRU_WIDTH), full2),
                  pl.BlockSpec(wi_bd.shape, full3), pl.BlockSpec((1, LRU_WIDTH), full2),
                  pl.BlockSpec((1, LRU_WIDTH), full2)],
        out_specs=[pl.BlockSpec((steps, batch, LRU_WIDTH), full3), pl.BlockSpec((batch, LRU_WIDTH), full2)],
        out_shape=[jax.ShapeDtypeStruct((steps, batch, LRU_WIDTH), F32),
                   jax.ShapeDtypeStruct((batch, LRU_WIDTH), F32)],
        compiler_params=_cparams(1),
        name="lru_decode",
    )(proj3, proj3, cbuf, h0, conv_w, conv_b.reshape(1, -1), wr_bd, b_r.reshape(1, -1), wi_bd,
      b_i.reshape(1, -1), lam.reshape(1, -1))


def _decode_bias(slopes_hq, q_times, window, dil, steps):
    heads, nq = slopes_hq.shape
    bias = np.zeros((heads, nq, window + LANES), np.float64)
    w = np.arange(window)
    for qi, t in enumerate(q_times):
        if t < 0:
            continue
        dist = window + t - w
        ok = (dist % dil == 0) & (dist <= window)
        bias[:, qi, :window] = np.where(ok[None], -slopes_hq[:, qi, None] * dist[None], -np.inf)
        for c in range(LANES):
            d = t - c
            good = c < steps and d >= 0 and d % dil == 0
            bias[:, qi, window + c] = -slopes_hq[:, qi] * d if good else -np.inf
    return jnp.asarray(bias.astype(np.float32))


def _decode_kernel(*refs, window, steps, has_sink, has_alias):
    refs = list(refs)
    q_ref, c_ref, new_ref, bias_ref = refs[:4]
    pos = 4
    sink_ref = None
    if has_sink:
        sink_ref = refs[pos]
        pos += 1
    if has_alias:
        pos += 1
    co_ref, o_ref, lse_ref = refs[pos:pos + 3]
    c = c_ref[0]
    bb, _, hb, hd, _ = c.shape
    nq = q_ref.shape[2]
    trow = lax.broadcasted_iota(jnp.int32, (SUBLANES, LANES), 0)
    tlane = lax.broadcasted_iota(jnp.int32, (SUBLANES, LANES), 1)
    place = jnp.where((trow == tlane) & (trow < steps), 1.0, 0.0).astype(F32)
    place = jnp.broadcast_to(place[None], (bb * 2 * hb, SUBLANES, LANES))
    new = jnp.einsum("htd,htl->hdl", new_ref[...].reshape(bb * 2 * hb, SUBLANES, hd), place,
                     preferred_element_type=F32, precision=lax.Precision.HIGHEST).reshape(bb, 2, hb, hd, LANES)
    merge = lambda a: a.reshape((bb * hb,) + a.shape[2:])
    k_ext = jnp.concatenate([merge(c[:, 0]), merge(new[:, 0])], axis=-1).astype(BF16)
    v_ext = jnp.concatenate([merge(c[:, 1]), merge(new[:, 1])], axis=-1).astype(BF16)
    q = merge(q_ref[...]).astype(BF16)
    bias = merge(jnp.broadcast_to(bias_ref[...][None], (bb,) + bias_ref.shape))
    s = jnp.einsum("hqd,hdw->hqw", q, k_ext, preferred_element_type=F32) * (HEAD_DIM ** -0.5) + bias
    m = jnp.max(s, axis=-1, keepdims=True)
    e = jnp.exp(s - m)
    l = jnp.sum(e, axis=-1, keepdims=True)
    lse = m + jnp.log(l)
    p = e / l
    if has_sink:
        sink = merge(jnp.broadcast_to(sink_ref[...][None], (bb,) + sink_ref.shape))
        p = p * jax.nn.sigmoid(lse - sink[:, :, 0:1])
    o = jnp.einsum("hqw,hdw->hqd", p.astype(BF16), v_ext, preferred_element_type=F32)
    o_ref[...] = o.reshape(bb, hb, nq, hd)
    lse_ref[...] = jnp.broadcast_to(lse, (bb * hb, nq, LANES)).reshape(bb, hb, nq, LANES)
    co_ref[0] = pltpu.roll(c, window - steps, 4)
    lane = lax.broadcasted_iota(jnp.int32, new.shape, 4)
    last = pltpu.roll(c[:, :, :, :, window - LANES:], LANES - steps, 4)
    tail = pltpu.roll(new, LANES - steps, 4)
    co_ref[0, :, :, :, :, window - LANES:] = jnp.where(lane >= LANES - steps, tail, last)


def _decode_attention(layer, cache_t, prev_out, q, new_front, bias, sink, rows_per_step, heads_per_step,
                      steps, name):
    n_layers, batch, _, heads, hd, window = cache_t.shape
    nq = q.shape[2]
    bb, hb = rows_per_step, heads_per_step
    cmap = lambda b, h: (layer, b, 0, h, 0, 0)
    in_specs = [pl.BlockSpec((bb, hb, nq, hd), lambda b, h: (b, h, 0, 0)),
                pl.BlockSpec((1, bb, 2, hb, hd, window), cmap),
                pl.BlockSpec((bb, 2, hb, SUBLANES, hd), lambda b, h: (b, 0, h, 0, 0)),
                pl.BlockSpec((hb, nq, window + LANES), lambda b, h: (h, 0, 0))]
    args = [q, cache_t, new_front, bias]
    if sink is not None:
        in_specs.append(pl.BlockSpec((hb, nq, LANES), lambda b, h: (h, 0, 0)))
        args.append(sink)
    aliases = {}
    if prev_out is not None:
        in_specs.append(pl.BlockSpec(memory_space=pl.ANY))
        aliases = {len(args): 0}
        args.append(prev_out)
    return pl.pallas_call(
        functools.partial(_decode_kernel, window=window, steps=steps, has_sink=sink is not None,
                          has_alias=prev_out is not None),
        grid=(batch // bb, heads // hb),
        in_specs=in_specs,
        out_specs=[pl.BlockSpec((1, bb, 2, hb, hd, window), cmap),
                   pl.BlockSpec((bb, hb, nq, hd), lambda b, h: (b, h, 0, 0)),
                   pl.BlockSpec((bb, hb, nq, LANES), lambda b, h: (b, h, 0, 0))],
        out_shape=[jax.ShapeDtypeStruct(cache_t.shape, F32),
                   jax.ShapeDtypeStruct((batch, heads, nq, hd), F32),
                   jax.ShapeDtypeStruct((batch, heads, nq, LANES), F32)],
        input_output_aliases=aliases,
        compiler_params=_cparams(2),
        name=name,
    )(*args)


def _router_kernel(x_ref, wt_ref, b_ref, e_ref, g_ref, c_ref):
    logits = lax.dot_general(wt_ref[...].astype(BF16), x_ref[...].astype(BF16), (((1,), (1,)), ((), ())),
                             preferred_element_type=F32)
    scores = jax.nn.sigmoid(logits)
    biased = scores + b_ref[...]
    tm = biased.shape[1]
    b3 = biased.reshape(N_EXPERT_GROUPS, GROUP_SIZE, tm)
    i3 = lax.broadcasted_iota(jnp.int32, b3.shape, 1).astype(F32)
    g1 = jnp.max(b3, axis=1, keepdims=True)
    first = jnp.min(jnp.where(b3 == g1, i3, float(GROUP_SIZE)), axis=1, keepdims=True)
    g2 = jnp.max(jnp.where(i3 == first, NEG_INF, b3), axis=1, keepdims=True)
    gs = g1 + g2
    gi = lax.broadcasted_iota(jnp.int32, gs.shape, 0).astype(F32)
    sel = jnp.zeros(gs.shape, F32)
    for _ in range(TOPK_GROUPS):
        m = jnp.max(gs, axis=0, keepdims=True)
        f = jnp.min(jnp.where(gs == m, gi, float(N_EXPERT_GROUPS)), axis=0, keepdims=True)
        hit = gi == f
        sel = jnp.where(hit, 1.0, sel)
        gs = jnp.where(hit, NEG_INF, gs)
    masked = jnp.where(sel > 0.5, b3, NEG_INF).reshape(N_EXPERTS, tm)
    ei = lax.broadcasted_iota(jnp.int32, masked.shape, 0).astype(F32)
    e_rows, g_rows = [], []
    chosen = jnp.zeros(masked.shape, F32)
    for _ in range(TOP_K):
        m = jnp.max(masked, axis=0, keepdims=True)
        f = jnp.min(jnp.where(masked == m, ei, float(N_EXPERTS)), axis=0, keepdims=True)
        hit = ei == f
        g_rows.append(jnp.sum(jnp.where(hit, scores, 0.0), axis=0, keepdims=True))
        e_rows.append(f)
        chosen = jnp.where(hit, 1.0, chosen)
        masked = jnp.where(hit, NEG_INF, masked)
    gates = jnp.concatenate(g_rows, axis=0)
    gates = gates / jnp.sum(gates, axis=0, keepdims=True) * ROUTED_SCALE
    e_ref[...] = jnp.concatenate(e_rows, axis=0).astype(jnp.int32)
    g_ref[...] = gates
    c_ref[0] = jnp.sum(chosen, axis=1, keepdims=True)


def _router(x, w_router_t, bias, tm):
    n = x.shape[0]
    nt = n // tm
    return pl.pallas_call(
        _router_kernel,
        grid=(nt,),
        in_specs=[pl.BlockSpec((tm, D_MODEL), lambda i: (i, 0)), pl.BlockSpec((N_EXPERTS, D_MODEL), lambda i: (0, 0)),
                  pl.BlockSpec((N_EXPERTS, 1), lambda i: (0, 0))],
        out_specs=[pl.BlockSpec((TOP_K, tm), lambda i: (0, i)), pl.BlockSpec((TOP_K, tm), lambda i: (0, i)),
                   pl.BlockSpec((1, N_EXPERTS, 1), lambda i: (i, 0, 0))],
        out_shape=[jax.ShapeDtypeStruct((TOP_K, n), jnp.int32), jax.ShapeDtypeStruct((TOP_K, n), F32),
                   jax.ShapeDtypeStruct((nt, N_EXPERTS, 1), F32)],
        compiler_params=_cparams(1),
        name="moe_router",
    )(x, w_router_t, bias.reshape(N_EXPERTS, 1))


KEY_EXPERT_SHIFT = 18
KEY_PAD_BIT = 17


def _dispatch_plan(top_e, gates, tile_counts, chunk, tm):
    n = top_e.shape[1]
    n_chunks = n // chunk
    pairs = TOP_K * chunk
    assert pairs < (1 << KEY_PAD_BIT) and tm <= (1 << KEY_PAD_BIT) and pairs % tm == 0
    per_chunk = lambda a: jnp.transpose(a.reshape(TOP_K, n_chunks, chunk), (1, 0, 2)).reshape(n_chunks, pairs)
    te, ga = per_chunk(top_e), per_chunk(gates)
    counts = tile_counts.reshape(n_chunks, -1, N_EXPERTS).sum(axis=1).astype(jnp.int32)
    n_pad = (-counts) % tm
    key_real = (te << KEY_EXPERT_SHIFT) + jnp.arange(pairs, dtype=jnp.int32)[None, :]
    e_pad = jnp.arange(N_EXPERTS, dtype=jnp.int32)[None, :, None]
    j_pad = jnp.arange(tm, dtype=jnp.int32)[None, None, :]
    key_pad = jnp.where(j_pad < n_pad[:, :, None],
                        (e_pad << KEY_EXPERT_SHIFT) + (1 << KEY_PAD_BIT) + j_pad,
                        (N_EXPERTS << KEY_EXPERT_SHIFT) + e_pad * tm + j_pad)
    keys = jnp.concatenate([key_real, key_pad.reshape(n_chunks, N_EXPERTS * tm)], axis=1)
    vals = jnp.concatenate([ga, jnp.zeros((n_chunks, N_EXPERTS * tm), F32)], axis=1)
    keys, vals = lax.sort((keys, vals), dimension=1, num_keys=1, is_stable=False)
    e_row = keys >> KEY_EXPERT_SHIFT
    real = ((keys >> KEY_PAD_BIT) & 1) == 0
    real = real & (e_row < N_EXPERTS)
    tok = (keys & ((1 << KEY_PAD_BIT) - 1)) % chunk
    nb = pairs // tm + N_EXPERTS
    gather_idx = jnp.where(real, tok, 0).reshape(n_chunks * nb, 1, tm)
    scatter_idx = jnp.where(real, tok, chunk).reshape(n_chunks * nb, 1, tm)
    scatter_idx = jnp.concatenate([scatter_idx, jnp.full((1, 1, tm), chunk, jnp.int32)], axis=0)
    block_e = jnp.minimum(e_row[:, ::tm], N_EXPERTS - 1).reshape(n_chunks * nb)
    used = ((counts + n_pad).sum(axis=1) // tm).astype(jnp.int32)
    return gather_idx, scatter_idx, vals.reshape(n_chunks * nb * tm, 1), block_e, used, nb


SCATTER_BATCH = 16


def _expert_kernel(be_ref, used_ref, gcur_ref, gnext_ref, scur_ref, sprev_ref, gate_ref, x_hbm, wg_ref, wu_ref,
                   wd_ref, y_hbm, x_s, y_s, xg, obuf, sem, *, tm, chunk):
    c = pl.program_id(0)
    j = pl.program_id(1)
    slot = j % 2
    used = used_ref[c]
    rows = chunk * ROW_TILES

    def gather(idx_ref, s):
        for r in range(tm):
            t = pl.multiple_of(idx_ref[0, 0, r] * ROW_TILES, ROW_TILES)
            xg[s, r * ROW_TILES:(r + 1) * ROW_TILES, :] = x_s[pl.ds(t, ROW_TILES), :]

    def scatter_add(idx_ref, s):
        for r0 in range(0, tm, SCATTER_BATCH):
            offs = [pl.multiple_of(idx_ref[0, 0, r0 + i] * ROW_TILES, ROW_TILES) for i in range(SCATTER_BATCH)]
            sums = [y_s[pl.ds(offs[i], ROW_TILES), :] + obuf[s, (r0 + i) * ROW_TILES:(r0 + i + 1) * ROW_TILES, :]
                    for i in range(SCATTER_BATCH)]
            for i in range(SCATTER_BATCH):
                y_s[pl.ds(offs[i], ROW_TILES), :] = sums[i]

    @pl.when(j == 0)
    def _():
        load = pltpu.make_async_copy(x_hbm.at[pl.ds(c * rows, rows)], x_s, sem.at[0])
        load.start()
        y_s[...] = jnp.zeros(y_s.shape, F32)

        @pl.when(c == 0)
        def _():
            obuf[...] = jnp.zeros(obuf.shape, F32)

        load.wait()
        gather(gcur_ref, slot)

    @pl.when(j < used)
    def _():
        gather(gnext_ref, 1 - slot)
        x = _load_tile_rows(xg.at[slot], tm).astype(BF16)
        hg = jnp.dot(x, wg_ref[0, 0].astype(BF16), preferred_element_type=F32)
        hu = jnp.dot(x, wu_ref[0, 0].astype(BF16), preferred_element_type=F32)
        hid = (hg * jax.nn.sigmoid(hg)) * hu
        out = jnp.dot(hid.astype(BF16), wd_ref[0, 0].astype(BF16), preferred_element_type=F32) * gate_ref[...]
        _store_tile_rows(obuf.at[slot], out, tm)
        scatter_add(sprev_ref, 1 - slot)

    @pl.when(j == used - 1)
    def _():
        scatter_add(scur_ref, slot)
        store = pltpu.make_async_copy(y_s.at[pl.ds(0, rows)], y_hbm.at[pl.ds(c * rows, rows)], sem.at[1])
        store.start()
        store.wait()


def _experts(x_tiles, plan, layer, w_gate, w_up, w_down, n_tokens, chunk, tm):
    gather_idx, scatter_idx, row_gate, block_e, used, nb = plan
    n_chunks = n_tokens // chunk
    dump_block = n_chunks * nb
    smem_blk = lambda f: pl.BlockSpec((1, 1, tm), f, memory_space=pltpu.SMEM)
    wmap = lambda c, j, be, us: (layer, be[c * nb + j], 0, 0)
    grid_spec = pltpu.PrefetchScalarGridSpec(
        num_scalar_prefetch=2,
        grid=(n_chunks, nb),
        in_specs=[smem_blk(lambda c, j, be, us: (c * nb + j, 0, 0)),
                  smem_blk(lambda c, j, be, us: (c * nb + jnp.minimum(j + 1, nb - 1), 0, 0)),
                  smem_blk(lambda c, j, be, us: (c * nb + j, 0, 0)),
                  smem_blk(lambda c, j, be, us: (jnp.where(j == 0, dump_block, c * nb + j - 1), 0, 0)),
                  pl.BlockSpec((tm, 1), lambda c, j, be, us: (c * nb + j, 0)),
                  pl.BlockSpec(memory_space=pl.ANY),
                  pl.BlockSpec((1, 1, D_MODEL, D_EXPERT), wmap), pl.BlockSpec((1, 1, D_MODEL, D_EXPERT), wmap),
                  pl.BlockSpec((1, 1, D_EXPERT, D_MODEL), wmap)],
        out_specs=pl.BlockSpec(memory_space=pl.ANY),
        scratch_shapes=[pltpu.VMEM((chunk * ROW_TILES, LANES), F32),
                        pltpu.VMEM(((chunk + 1) * ROW_TILES, LANES), F32),
                        pltpu.VMEM((2, tm * ROW_TILES, LANES), F32), pltpu.VMEM((2, tm * ROW_TILES, LANES), F32),
                        pltpu.SemaphoreType.DMA((2,))],
    )
    return pl.pallas_call(
        functools.partial(_expert_kernel, tm=tm, chunk=chunk),
        grid_spec=grid_spec,
        out_shape=jax.ShapeDtypeStruct((n_tokens * ROW_TILES, LANES), F32),
        compiler_params=_cparams(2),
        name="moe_experts",
    )(block_e, used, gather_idx, gather_idx, scatter_idx, scatter_idx, row_gate, x_tiles, w_gate, w_up, w_down)


def _combine_kernel(y_ref, x_ref, wsg_ref, wsu_ref, wsd_ref, g_ref, b_ref, o_ref, wsg_bf, wsu_bf, wsd_bf, *, tm):
    @pl.when(pl.program_id(0) == 0)
    def _():
        wsg_bf[...] = wsg_ref[...].astype(BF16)
        wsu_bf[...] = wsu_ref[...].astype(BF16)
        wsd_bf[...] = wsd_ref[...].astype(BF16)

    routed = _load_tile_rows(y_ref, tm)
    x = x_ref[...]
    xb = x.astype(BF16)
    sg = jnp.dot(xb, wsg_bf[...], preferred_element_type=F32)
    su = jnp.dot(xb, wsu_bf[...], preferred_element_type=F32)
    shared = jnp.dot(((sg * jax.nn.sigmoid(sg)) * su).astype(BF16), wsd_bf[...], preferred_element_type=F32)
    o_ref[...] = _layer_norm(DEEPNORM_ALPHA * x + (routed + shared), g_ref[...], b_ref[...])


def _combine(y_routed, x, ws_gate, ws_up, ws_down, g, b, tm):
    n = x.shape[0]
    fixed = lambda i: (0, 0)
    return pl.pallas_call(
        functools.partial(_combine_kernel, tm=tm),
        grid=(n // tm,),
        in_specs=[pl.BlockSpec((tm * ROW_TILES, LANES), lambda i: (i, 0)), pl.BlockSpec((tm, D_MODEL), lambda i: (i, 0)),
                  pl.BlockSpec(ws_gate.shape, fixed), pl.BlockSpec(ws_up.shape, fixed),
                  pl.BlockSpec(ws_down.shape, fixed),
                  pl.BlockSpec((1, D_MODEL), fixed), pl.BlockSpec((1, D_MODEL), fixed)],
        out_specs=pl.BlockSpec((tm, D_MODEL), lambda i: (i, 0)),
        out_shape=jax.ShapeDtypeStruct((n, D_MODEL), F32),
        scratch_shapes=[pltpu.VMEM(ws_gate.shape, BF16), pltpu.VMEM(ws_up.shape, BF16),
                        pltpu.VMEM(ws_down.shape, BF16)],
        compiler_params=_cparams(1),
        name="moe_combine",
    )(y_routed, x, ws_gate, ws_up, ws_down, g.reshape(1, -1), b.reshape(1, -1))


def _moe_block(x, x_tiles, layer, w_router_t, router_bias, w_gate, w_up, w_down, ws_gate, ws_up, ws_down, g, b,
               tm_route, chunk, tm_expert, tm_combine):
    n = x.shape[0]
    top_e, gates, tile_counts = _router(x, w_router_t, router_bias, tm_route)
    plan = _dispatch_plan(top_e, gates, tile_counts, chunk, tm_expert)
    y_routed = _experts(x_tiles, plan, layer, w_gate, w_up, w_down, n, chunk, tm_expert)
    return _combine(y_routed, x, ws_gate, ws_up, ws_down, g, b, tm_combine)


def _to_cache_t(cache):
    return jnp.transpose(cache, (0, 1, 3, 4, 5, 2))


def _from_cache_t(cache_t):
    return jnp.transpose(cache_t, (0, 1, 5, 2, 3, 4))


def _new_kv_columns(k, v, steps, batch, heads):
    kv = jnp.stack([k, v]).reshape(2, steps, batch, heads, HEAD_DIM)
    kv = jnp.transpose(kv, (2, 0, 3, 1, 4))
    return jnp.pad(kv, ((0, 0),) * 3 + ((0, SUBLANES - steps), (0, 0)))


def kernel(x_prompt, x_sample, state_lru_h, state_lru_conv, cache_swa_kv, cache_dil1_kv, cache_dil2_kv,
           cache_dil3_kv, w_in_even, conv_w, conv_b, w_rgate, b_rgate, w_igate, b_igate, lru_lambda, swa_sinks,
           w_out_even, w_in_odd, w_out_odd, ln1_g, ln1_b, ln2_g, ln2_b, w_router, router_bias, w_exp_gate,
           w_exp_up, w_exp_down, w_sh_gate, w_sh_up, w_sh_down):
    batch, seq, _ = x_prompt.shape
    dbatch, steps, _ = x_sample.shape
    n_p = batch * seq
    n_s = dbatch * steps
    xp = x_prompt.reshape(n_p, D_MODEL)
    xs = jnp.transpose(x_sample, (1, 0, 2)).reshape(n_s, D_MODEL)

    slopes8 = _alibi_slopes(SWA_HEADS)
    pad_q = SUBLANES - steps
    q_times_dil = list(range(steps)) + [-1] * pad_q
    swa_q_times = [t for t in range(steps) for _ in range(SWA_GROUP)]
    swa_slopes = np.stack([np.tile(slopes8[kv * SWA_GROUP:(kv + 1) * SWA_GROUP], steps)
                           for kv in range(SWA_KV_HEADS)])
    swa_bias = _decode_bias(swa_slopes, swa_q_times, SWA_WINDOW, 1, steps)
    dil_bias = [_decode_bias(np.tile(_alibi_slopes(DIL_HEADS)[:, None], (1, SUBLANES)), q_times_dil,
                             DIL_WINDOWS[g], DIL_RATES[g], steps) for g in range(N_DIL)]

    et_flat = _head_expand_matrix(DIL_HEADS, DIL_HEADS)
    et_fold = _head_expand_matrix(DIL_HEADS, FOLD_HEADS)
    ets_prompt = [et_flat if d == 1 else et_fold for d in DIL_RATES]
    ets_sample = [et_flat] * N_DIL

    swa_t = _to_cache_t(cache_swa_kv)
    dil_t = [_to_cache_t(c) for c in (cache_dil1_kv, cache_dil2_kv, cache_dil3_kv)]
    swa_out = None
    dil_out = [None] * N_DIL
    lru_h_p, lru_h_s, lru_c_p, lru_c_s, swa_p = [], [], [], [], []
    dil_p = [[] for _ in range(N_DIL)]

    for l in range(DEPTH):
        j = l // 2
        if l % 2 == 0:
            wr_bd, wi_bd = _gate_blockdiag(w_rgate[j]), _gate_blockdiag(w_igate[j])
            proj = _matmul(xp, w_in_even[j], 1024, 896, "inproj_even_prompt")
            proj3 = proj.reshape(batch, seq, -1)
            a_out, h_last = _lru_prompt(proj3, conv_w[j], conv_b[j], wr_bd, b_rgate[j], wi_bd, b_igate[j],
                                        lru_lambda[j])
            (o_swa,) = _banded_attention(proj3, batch, seq, 1, (0, 2), (0, 12), (0, 13), SWA_KV_HEADS * HEAD_DIM,
                                         SWA_HEADS, SWA_GROUP, slopes8, 1, swa_sinks[j], "swa_prompt")
            lru_h_p.append(h_last[:, 0])
            lru_c_p.append(proj3[:, seq - (CONV_W - 1):, :LRU_WIDTH])
            keep = min(SWA_WINDOW, seq)
            swa_p.append(proj3[:, seq - keep:, 2 * LRU_WIDTH + SWA_HEADS * HEAD_DIM:]
                         .reshape(batch, keep, 2, SWA_KV_HEADS, HEAD_DIM))
            xp, xp_tiles = _outproj_even(xp, a_out.reshape(n_p, -1), o_swa.reshape(n_p, -1), w_out_even[j],
                                         ln1_g[l], ln1_b[l], 512)
            proj_s = _matmul(xs, w_in_even[j], n_s, 896, "inproj_even_sample")
            proj_s3 = proj_s.reshape(steps, dbatch, -1)
            cbuf = jnp.transpose(state_lru_conv[j], (1, 0, 2))
            a_s, h_s = _lru_decode(proj_s3, cbuf, state_lru_h[j], conv_w[j], conv_b[j], wr_bd, b_rgate[j], wi_bd,
                                   b_igate[j], lru_lambda[j])
            lru_h_s.append(h_s)
            xr_s = proj_s3[:, :, :LRU_WIDTH]
            lru_c_s.append(jnp.transpose(jnp.concatenate([cbuf, xr_s], axis=0)[-(CONV_W - 1):], (1, 0, 2)))
            o0 = 2 * LRU_WIDTH
            q_s = proj_s[:, o0:o0 + SWA_HEADS * HEAD_DIM].reshape(steps, dbatch, SWA_KV_HEADS, SWA_GROUP, HEAD_DIM)
            q_s = jnp.transpose(q_s, (1, 2, 0, 3, 4)).reshape(dbatch, SWA_KV_HEADS, steps * SWA_GROUP, HEAD_DIM)
            o1 = o0 + SWA_HEADS * HEAD_DIM
            o2 = o1 + SWA_KV_HEADS * HEAD_DIM
            front = _new_kv_columns(proj_s[:, o1:o2], proj_s[:, o2:], steps, dbatch, SWA_KV_HEADS)
            sink = jnp.broadcast_to(
                jnp.tile(swa_sinks[j].reshape(SWA_KV_HEADS, SWA_GROUP), (1, steps))[:, :, None],
                (SWA_KV_HEADS, steps * SWA_GROUP, LANES))
            swa_out, o_dec, _ = _decode_attention(j, swa_t, swa_out, q_s, front, swa_bias, sink,
                                                  min(dbatch, 16), SWA_KV_HEADS, steps, "swa_decode")
            o_dec = o_dec.reshape(dbatch, SWA_KV_HEADS, steps, SWA_GROUP, HEAD_DIM)
            o_dec = jnp.transpose(o_dec, (2, 0, 1, 3, 4)).reshape(n_s, SWA_HEADS * HEAD_DIM)
            xs, xs_tiles = _outproj_even(xs, a_s.reshape(n_s, -1), o_dec, w_out_even[j], ln1_g[l], ln1_b[l], n_s)
        else:
            proj = _matmul(xp, w_in_odd[j], 1024, 768, "inproj_odd_prompt")
            row_w = proj.shape[1]
            proj3 = proj.reshape(batch, seq, row_w)
            os_, ls_ = [], []
            for g in range(N_DIL):
                dil = DIL_RATES[g]
                if dil == 1:
                    n_blk = row_w // (DIL_HEADS * HEAD_DIM)
                    o_g, lse_g = _banded_attention(proj3, batch, seq, 1, (n_blk, 3 * g), (n_blk, 3 * g + 1),
                                                   (n_blk, 3 * g + 2), DIL_HEADS * HEAD_DIM, DIL_HEADS, 1,
                                                   _alibi_slopes(DIL_HEADS), dil, None, "dil_prompt")
                else:
                    o_g, lse_g = _fold_attention(proj3, g, dil, _alibi_slopes(DIL_HEADS))
                os_.append(o_g.reshape(n_p, -1))
                ls_.append(lse_g.reshape(n_p, -1))
                keep = min(DIL_WINDOWS[g], seq)
                c0 = g * 3 * DIL_HEADS * HEAD_DIM + DIL_HEADS * HEAD_DIM
                dil_p[g].append(proj3[:, seq - keep:, c0:c0 + 2 * DIL_HEADS * HEAD_DIM]
                                .reshape(batch, keep, 2, DIL_HEADS, HEAD_DIM))
            xp, xp_tiles = _outproj_odd(xp, os_, ls_, ets_prompt, w_out_odd[j], ln1_g[l], ln1_b[l], 512)
            proj_s = _matmul(xs, w_in_odd[j], n_s, 768, "inproj_odd_sample")
            os_, ls_ = [], []
            gw = DIL_HEADS * HEAD_DIM
            for g in range(N_DIL):
                c0 = g * 3 * gw
                q_s = proj_s[:, c0:c0 + gw].reshape(steps, dbatch, DIL_HEADS, HEAD_DIM)
                q_s = jnp.pad(jnp.transpose(q_s, (1, 2, 0, 3)), ((0, 0), (0, 0), (0, pad_q), (0, 0)))
                front = _new_kv_columns(proj_s[:, c0 + gw:c0 + 2 * gw], proj_s[:, c0 + 2 * gw:c0 + 3 * gw],
                                        steps, dbatch, DIL_HEADS)
                rows_per_step, heads_per_step = ((8, 8), (2, 8), (1, 4))[g]
                dil_out[g], o_dec, lse_dec = _decode_attention(j, dil_t[g], dil_out[g], q_s, front,
                                                               dil_bias[g], None, min(dbatch, rows_per_step),
                                                               heads_per_step, steps, "dil_decode")
                o_dec = jnp.transpose(o_dec[:, :, :steps], (2, 0, 1, 3)).reshape(n_s, gw)
                lse_dec = jnp.transpose(lse_dec[:, :, :steps, 0], (2, 0, 1)).reshape(n_s, DIL_HEADS)
                os_.append(o_dec)
                ls_.append(jnp.pad(lse_dec, ((0, 0), (0, LANES - DIL_HEADS))))
            xs, xs_tiles = _outproj_odd(xs, os_, ls_, ets_sample, w_out_odd[j], ln1_g[l], ln1_b[l], n_s)

        moe_w = (l, jnp.transpose(w_router[l]), router_bias[l], w_exp_gate, w_exp_up, w_exp_down,
                 w_sh_gate[l], w_sh_up[l], w_sh_down[l], ln2_g[l], ln2_b[l])
        xp = _moe_block(xp, xp_tiles, *moe_w, tm_route=512, chunk=min(n_p, 4096), tm_expert=256, tm_combine=512)
        xs = _moe_block(xs, xs_tiles, *moe_w, tm_route=n_s, chunk=n_s, tm_expert=128, tm_combine=n_s)

    y_prompt = xp.reshape(batch, seq, D_MODEL)
    y_sample = jnp.transpose(xs.reshape(steps, dbatch, D_MODEL), (1, 0, 2))
    return (y_prompt, y_sample, jnp.stack(lru_h_p), jnp.stack(lru_h_s), jnp.stack(lru_c_p), jnp.stack(lru_c_s),
            jnp.stack(swa_p), _from_cache_t(swa_out),
            jnp.stack(dil_p[0]), _from_cache_t(dil_out[0]), jnp.stack(dil_p[1]), _from_cache_t(dil_out[1]),
            jnp.stack(dil_p[2]), _from_cache_t(dil_out[2]))
```

```python
import functools

import numpy as np
import jax
import jax.numpy as jnp
from jax import lax
from jax.experimental import pallas as pl
from jax.experimental.pallas import tpu as pltpu

F32 = jnp.float32
BF16 = jnp.bfloat16

D_MODEL = 1024
HEAD_DIM = 64
LRU_WIDTH = 512
LRU_C = 8.0
CONV_W = 4
SWA_HEADS = 8
SWA_KV_HEADS = 2
SWA_GROUP = SWA_HEADS // SWA_KV_HEADS
SWA_WINDOW = 128
DIL_WINDOWS = (128, 512, 2048)
DIL_RATES = (1, 4, 16)
N_DIL = 3
DIL_HEADS = 8
BAND = 128
N_EXPERTS = 64
N_EXPERT_GROUPS = 8
GROUP_SIZE = N_EXPERTS // N_EXPERT_GROUPS
TOPK_GROUPS = 4
TOP_K = 8
D_EXPERT = 256
ROUTED_SCALE = 2.5
LN_EPS = 1e-5
DEPTH = 4
DEEPNORM_ALPHA = (2 * DEPTH) ** 0.25

LANES = 128
SUBLANES = 8
ROW_TILES = D_MODEL // LANES
VMEM_LIMIT = 48 * 1024 * 1024

NEG_INF = float("-inf")


def _cparams(n_grid_dims):
    return pltpu.CompilerParams(dimension_semantics=("arbitrary",) * n_grid_dims, vmem_limit_bytes=VMEM_LIMIT)


def _alibi_slopes(n):
    return np.exp2(-8.0 * np.arange(1, n + 1, dtype=np.float64) / n)


def _bdot(a, b):
    return jnp.dot(a.astype(BF16), b.astype(BF16), preferred_element_type=F32)


def _layer_norm(z, g, b):
    zc = z - jnp.mean(z, -1, keepdims=True)
    var = jnp.mean(zc * zc, -1, keepdims=True)
    return zc * lax.rsqrt(var + LN_EPS) * g + b


def _store_tile_rows(ref, val, rows):
    for c in range(ROW_TILES):
        ref[pl.ds(c, rows, stride=ROW_TILES), :] = val[:, c * LANES:(c + 1) * LANES]


def _load_tile_rows(ref, rows):
    return jnp.concatenate([ref[pl.ds(c, rows, stride=ROW_TILES), :] for c in range(ROW_TILES)], axis=-1)


def _mm_kernel(x_ref, w_ref, o_ref):
    o_ref[...] = _bdot(x_ref[...], w_ref[...])


def _matmul(x, w, tm, tn, name):
    m, k = x.shape
    n = w.shape[1]
    return pl.pallas_call(
        _mm_kernel,
        grid=(m // tm, n // tn),
        in_specs=[pl.BlockSpec((tm, k), lambda i, j: (i, 0)), pl.BlockSpec((k, tn), lambda i, j: (0, j))],
        out_specs=pl.BlockSpec((tm, tn), lambda i, j: (i, j)),
        out_shape=jax.ShapeDtypeStruct((m, n), F32),
        compiler_params=_cparams(2),
        name=name,
    )(x, w)


def _outproj_even_kernel(x_ref, a_ref, o_ref, w_ref, g_ref, b_ref, y_ref, yt_ref, wbf_ref):
    @pl.when(pl.program_id(0) == 0)
    def _():
        wbf_ref[...] = w_ref[...].astype(BF16)

    half = a_ref.shape[1]
    f = jnp.dot(a_ref[...].astype(BF16), wbf_ref[0:half, :], preferred_element_type=F32)
    f = f + jnp.dot(o_ref[...].astype(BF16), wbf_ref[half:, :], preferred_element_type=F32)
    y = _layer_norm(DEEPNORM_ALPHA * x_ref[...] + f, g_ref[...], b_ref[...])
    y_ref[...] = y
    _store_tile_rows(yt_ref, y, y.shape[0])


def _outproj_even(x, a, o, w, g, b, tm):
    m = x.shape[0]
    row = lambda i: (i, 0)
    fixed = lambda i: (0, 0)
    return pl.pallas_call(
        _outproj_even_kernel,
        grid=(m // tm,),
        in_specs=[pl.BlockSpec((tm, D_MODEL), row), pl.BlockSpec((tm, a.shape[1]), row),
                  pl.BlockSpec((tm, o.shape[1]), row), pl.BlockSpec(w.shape, fixed),
                  pl.BlockSpec((1, D_MODEL), fixed), pl.BlockSpec((1, D_MODEL), fixed)],
        out_specs=[pl.BlockSpec((tm, D_MODEL), row), pl.BlockSpec((tm * ROW_TILES, LANES), row)],
        out_shape=[jax.ShapeDtypeStruct((m, D_MODEL), F32), jax.ShapeDtypeStruct((m * ROW_TILES, LANES), F32)],
        scratch_shapes=[pltpu.VMEM(w.shape, BF16)],
        compiler_params=_cparams(1),
        name="outproj_even",
    )(x, a, o, w, g.reshape(1, -1), b.reshape(1, -1))


def _outproj_odd_kernel(x_ref, o1_ref, o2_ref, o3_ref, l1_ref, l2_ref, l3_ref, e1_ref, e2_ref, e3_ref, w_ref, g_ref,
                        b_ref, y_ref, yt_ref, wbf_ref):
    @pl.when(pl.program_id(0) == 0)
    def _():
        wbf_ref[...] = w_ref[...].astype(BF16)

    def expand(l_ref, e_ref):
        lv = l_ref[...]
        hi = lv.astype(BF16)
        r1 = lv - hi.astype(F32)
        mid = r1.astype(BF16)
        lo = (r1 - mid.astype(F32)).astype(BF16)
        e = e_ref[...]
        dot = lambda a: jnp.dot(a, e, preferred_element_type=F32)
        return (dot(hi) + dot(mid)) + dot(lo)

    l1, l2, l3 = expand(l1_ref, e1_ref), expand(l2_ref, e2_ref), expand(l3_ref, e3_ref)
    m = jnp.maximum(jnp.maximum(l1, l2), l3)
    w1, w2, w3 = jnp.exp(l1 - m), jnp.exp(l2 - m), jnp.exp(l3 - m)
    mix = (w1 * o1_ref[...] + w2 * o2_ref[...] + w3 * o3_ref[...]) / (w1 + w2 + w3)
    f = jnp.dot(mix.astype(BF16), wbf_ref[...], preferred_element_type=F32)
    y = _layer_norm(DEEPNORM_ALPHA * x_ref[...] + f, g_ref[...], b_ref[...])
    y_ref[...] = y
    _store_tile_rows(yt_ref, y, y.shape[0])


def _head_expand_matrix(n_heads, heads_per_tile):
    n_tiles = n_heads // heads_per_tile
    et = np.zeros((n_tiles * LANES, n_heads * HEAD_DIM), np.float32)
    for h in range(n_heads):
        et[(h // heads_per_tile) * LANES + h % heads_per_tile, h * HEAD_DIM:(h + 1) * HEAD_DIM] = 1.0
    return jnp.asarray(et, dtype=BF16)


def _outproj_odd(x, os_, ls_, ets, w, g, b, tm):
    m = x.shape[0]
    row = lambda i: (i, 0)
    fixed = lambda i: (0, 0)
    width = os_[0].shape[1]
    return pl.pallas_call(
        _outproj_odd_kernel,
        grid=(m // tm,),
        in_specs=[pl.BlockSpec((tm, D_MODEL), row)] + [pl.BlockSpec((tm, width), row)] * 3
                 + [pl.BlockSpec((tm, l.shape[1]), row) for l in ls_]
                 + [pl.BlockSpec(e.shape, fixed) for e in ets]
                 + [pl.BlockSpec(w.shape, fixed), pl.BlockSpec((1, D_MODEL), fixed), pl.BlockSpec((1, D_MODEL), fixed)],
        out_specs=[pl.BlockSpec((tm, D_MODEL), row), pl.BlockSpec((tm * ROW_TILES, LANES), row)],
        out_shape=[jax.ShapeDtypeStruct((m, D_MODEL), F32), jax.ShapeDtypeStruct((m * ROW_TILES, LANES), F32)],
        scratch_shapes=[pltpu.VMEM(w.shape, BF16)],
        compiler_params=_cparams(1),
        name="outproj_odd",
    )(x, *os_, *ls_, *ets, w, g.reshape(1, -1), b.reshape(1, -1))


def _band_bias(slopes, dist_scale, window):
    qi = np.arange(BAND)[:, None]
    kj = np.arange(2 * BAND)[None, :]
    dist = qi + BAND - kj
    valid = (dist >= 0) & (dist <= window)
    bias = -slopes[:, None, None] * (dist_scale * dist)[None].astype(np.float64)
    return jnp.asarray(np.where(valid[None], bias, -np.inf).astype(np.float32))


def _band_kernel(*refs, n_heads, group, has_sink):
    if has_sink:
        q_ref, kp_ref, kc_ref, vp_ref, vc_ref, bias_ref, sink_ref, o_ref = refs
    else:
        q_ref, kp_ref, kc_ref, vp_ref, vc_ref, bias_ref, o_ref, lse_ref = refs
    j = pl.program_id(2)
    q = q_ref[0]
    k = jnp.concatenate([kp_ref[0], kc_ref[0]], axis=0)
    v = jnp.concatenate([vp_ref[0], vc_ref[0]], axis=0)
    col = lax.broadcasted_iota(jnp.int32, (BAND, 2 * BAND), 1)
    keep = (col >= BAND) | (j > 0)
    lane = lax.broadcasted_iota(jnp.int32, (BAND, LANES), 1)
    lse_all = jnp.zeros((BAND, LANES), F32)
    for h in range(n_heads):
        kh = h // group
        qh = q[:, h * HEAD_DIM:(h + 1) * HEAD_DIM].astype(BF16)
        kk = k[:, kh * HEAD_DIM:(kh + 1) * HEAD_DIM].astype(BF16)
        vv = v[:, kh * HEAD_DIM:(kh + 1) * HEAD_DIM].astype(BF16)
        s = lax.dot_general(qh, kk, (((1,), (1,)), ((), ())), preferred_element_type=F32)
        s = s * (HEAD_DIM ** -0.5) + bias_ref[h]
        s = jnp.where(keep, s, NEG_INF)
        m = jnp.max(s, axis=-1, keepdims=True)
        e = jnp.exp(s - m)
        l = jnp.sum(e, axis=-1, keepdims=True)
        o = jnp.dot(e.astype(BF16), vv, preferred_element_type=F32) / l
        lse = m + jnp.log(l)
        if has_sink:
            o = o * jax.nn.sigmoid(lse - sink_ref[h])
        else:
            lse_all = jnp.where(lane == h, lse, lse_all)
        o_ref[0, :, h * HEAD_DIM:(h + 1) * HEAD_DIM] = o
    if not has_sink:
        lse_ref[0] = lse_all


def _banded_attention(src, batch, lf, fold, qcol, kcol, vcol, kv_width, n_heads, group, slopes, dist_scale,
                      sinks, name):
    nb = lf // BAND
    qw = n_heads * HEAD_DIM
    bias = _band_bias(slopes, dist_scale, BAND)
    qmap = lambda b, r, j: (b, j, r * qcol[0] + qcol[1])
    kprev = lambda b, r, j: (b, jnp.maximum(j - 1, 0), r * kcol[0] + kcol[1])
    kcur = lambda b, r, j: (b, j, r * kcol[0] + kcol[1])
    vprev = lambda b, r, j: (b, jnp.maximum(j - 1, 0), r * vcol[0] + vcol[1])
    vcur = lambda b, r, j: (b, j, r * vcol[0] + vcol[1])
    in_specs = [pl.BlockSpec((1, BAND, qw), qmap),
                pl.BlockSpec((1, BAND, kv_width), kprev), pl.BlockSpec((1, BAND, kv_width), kcur),
                pl.BlockSpec((1, BAND, kv_width), vprev), pl.BlockSpec((1, BAND, kv_width), vcur),
                pl.BlockSpec(bias.shape, lambda b, r, j: (0, 0, 0))]
    args = [src, src, src, src, src, bias]
    omap = lambda b, r, j: (b, j, r)
    out_specs = [pl.BlockSpec((1, BAND, qw), omap)]
    out_shape = [jax.ShapeDtypeStruct((batch, lf, fold * qw), F32)]
    if sinks is not None:
        in_specs.append(pl.BlockSpec(memory_space=pltpu.SMEM))
        args.append(sinks)
    else:
        out_specs.append(pl.BlockSpec((1, BAND, LANES), omap))
        out_shape.append(jax.ShapeDtypeStruct((batch, lf, fold * LANES), F32))
    return pl.pallas_call(
        functools.partial(_band_kernel, n_heads=n_heads, group=group, has_sink=sinks is not None),
        grid=(batch, fold, nb),
        in_specs=in_specs, out_specs=out_specs, out_shape=out_shape,
        compiler_params=_cparams(3),
        name=name,
    )(*args)


FOLD_HEADS = LANES // HEAD_DIM


def _fold_band_kernel(*refs, dil, has_prev):
    if has_prev:
        q_ref, kp_ref, kc_ref, vp_ref, vc_ref, bias_ref, o_ref, lse_ref, q_s, k_s, v_s, o_s, lse_s = refs
    else:
        q_ref, kc_ref, vc_ref, bias_ref, o_ref, lse_ref, q_s, k_s, v_s, o_s, lse_s = refs
    j = pl.program_id(1)
    hh = pl.program_id(2)
    kr = 2 * BAND if has_prev else BAND
    for rho in range(dil):
        q_s[rho * BAND:(rho + 1) * BAND, :] = q_ref[0, pl.ds(rho, BAND, stride=dil), :]
        if has_prev:
            k_s[rho * kr:rho * kr + BAND, :] = kp_ref[0, pl.ds(rho, BAND, stride=dil), :]
            v_s[rho * kr:rho * kr + BAND, :] = vp_ref[0, pl.ds(rho, BAND, stride=dil), :]
        k_s[rho * kr + kr - BAND:(rho + 1) * kr, :] = kc_ref[0, pl.ds(rho, BAND, stride=dil), :]
        v_s[rho * kr + kr - BAND:(rho + 1) * kr, :] = vc_ref[0, pl.ds(rho, BAND, stride=dil), :]
    col = lax.broadcasted_iota(jnp.int32, (BAND, kr), 1)
    keep = (col >= kr - BAND) | (j > 0)
    lane = lax.broadcasted_iota(jnp.int32, (BAND, LANES), 1)

    def body(rho, carry):
        q = q_s[pl.ds(pl.multiple_of(rho * BAND, BAND), BAND), :]
        k = k_s[pl.ds(pl.multiple_of(rho * kr, BAND), kr), :]
        v = v_s[pl.ds(pl.multiple_of(rho * kr, BAND), kr), :]
        outs = []
        lse_all = jnp.zeros((BAND, LANES), F32)
        for h in range(FOLD_HEADS):
            sl = slice(h * HEAD_DIM, (h + 1) * HEAD_DIM)
            s = lax.dot_general(q[:, sl].astype(BF16), k[:, sl].astype(BF16), (((1,), (1,)), ((), ())),
                                preferred_element_type=F32)
            s = s * (HEAD_DIM ** -0.5) + bias_ref[hh * FOLD_HEADS + h]
            if has_prev:
                s = jnp.where(keep, s, NEG_INF)
            m = jnp.max(s, axis=-1, keepdims=True)
            e = jnp.exp(s - m)
            l = jnp.sum(e, axis=-1, keepdims=True)
            outs.append(jnp.dot(e.astype(BF16), v[:, sl].astype(BF16), preferred_element_type=F32) / l)
            lse_all = jnp.where(lane == h, m + jnp.log(l), lse_all)
        o_s[pl.ds(pl.multiple_of(rho * BAND, BAND), BAND), :] = jnp.concatenate(outs, axis=-1)
        lse_s[pl.ds(pl.multiple_of(rho * BAND, BAND), BAND), :] = lse_all
        return carry

    lax.fori_loop(0, dil, body, 0, unroll=2)
    for rho in range(dil):
        o_ref[0, pl.ds(rho, BAND, stride=dil), :] = o_s[rho * BAND:(rho + 1) * BAND, :]
        lse_ref[0, pl.ds(rho, BAND, stride=dil), :] = lse_s[rho * BAND:(rho + 1) * BAND, :]


def _fold_attention(proj3, group, dil, slopes):
    batch, seq, _ = proj3.shape
    rows = BAND * dil
    nch = seq // rows
    has_prev = nch > 1
    width = FOLD_HEADS * HEAD_DIM
    n_hh = DIL_HEADS // FOLD_HEADS
    base = group * 3 * n_hh
    bias = _band_bias(slopes, dil, BAND)
    if not has_prev:
        bias = bias[:, :, BAND:]
    blk = (1, rows, width)
    cur = lambda part: (lambda b, j, h: (b, j, base + part * n_hh + h))
    prev = lambda part: (lambda b, j, h: (b, jnp.maximum(j - 1, 0), base + part * n_hh + h))
    in_specs = [pl.BlockSpec(blk, cur(0))]
    args = [proj3]
    for part in (1, 2):
        if has_prev:
            in_specs.append(pl.BlockSpec(blk, prev(part)))
            args.append(proj3)
        in_specs.append(pl.BlockSpec(blk, cur(part)))
        args.append(proj3)
    in_specs.append(pl.BlockSpec(bias.shape, lambda b, j, h: (0, 0, 0)))
    args.append(bias)
    kr = 2 * BAND if has_prev else BAND
    omap = lambda b, j, h: (b, j, h)
    return pl.pallas_call(
        functools.partial(_fold_band_kernel, dil=dil, has_prev=has_prev),
        grid=(batch, nch, n_hh),
        in_specs=in_specs,
        out_specs=[pl.BlockSpec(blk, omap), pl.BlockSpec((1, rows, LANES), omap)],
        out_shape=[jax.ShapeDtypeStruct((batch, seq, DIL_HEADS * HEAD_DIM), F32),
                   jax.ShapeDtypeStruct((batch, seq, n_hh * LANES), F32)],
        scratch_shapes=[pltpu.VMEM((rows, width), F32), pltpu.VMEM((dil * kr, width), F32),
                        pltpu.VMEM((dil * kr, width), F32), pltpu.VMEM((rows, width), F32),
                        pltpu.VMEM((rows, LANES), F32)],
        compiler_params=_cparams(3),
        name="dil_prompt_fold",
    )(*args)


def _gelu_tanh(x):
    return 0.5 * x * (1.0 + jnp.tanh(0.7978845608028654 * (x + 0.044715 * x * x * x)))


def _softplus(z):
    return jnp.maximum(z, 0.0) + jnp.log1p(jnp.exp(-jnp.abs(z)))


def _lru_gates(xc, wr, br, wi, bi, lam):
    r = jax.nn.sigmoid(_bdot(xc, wr) + br)
    i = jax.nn.sigmoid(_bdot(xc, wi) + bi)
    log_a = -LRU_C * r * _softplus(-lam)
    a = jnp.exp(log_a)
    th = jnp.tanh(log_a)
    u = jnp.sqrt(-2.0 * th / (1.0 - th)) * (i * xc)
    return a, u


def _lru_prompt_kernel(xr_ref, yg_ref, cw_ref, cb_ref, wr_ref, br_ref, wi_ref, bi_ref, lam_ref,
                       aout_ref, hlast_ref, xpad_ref, a_ref, u_ref):
    length = xr_ref.shape[1]
    xr = xr_ref[0]
    xpad_ref[0:SUBLANES, :] = jnp.zeros((SUBLANES, LANES), F32)
    xpad_ref[SUBLANES:, :] = xr
    cw = cw_ref[...]
    xc = cb_ref[...] + cw[CONV_W - 1:CONV_W, :] * xr
    for j in range(CONV_W - 1):
        shift = CONV_W - 1 - j
        xc = xc + cw[j:j + 1, :] * xpad_ref[pl.ds(SUBLANES - shift, length), :]
    a, u = _lru_gates(xc, wr_ref[0], br_ref[...], wi_ref[0], bi_ref[...], lam_ref[...])
    a_ref[...] = a
    u_ref[...] = u
    row = lax.broadcasted_iota(jnp.int32, (SUBLANES, LANES), 0)

    def body(blk, carry):
        off = pl.multiple_of(blk * SUBLANES, SUBLANES)
        ab = a_ref[pl.ds(off, SUBLANES), :]
        ub = u_ref[pl.ds(off, SUBLANES), :]
        for d in (1, 2, 4):
            ush = jnp.where(row >= d, pltpu.roll(ub, d, 0), 0.0)
            ash = jnp.where(row >= d, pltpu.roll(ab, d, 0), 1.0)
            ub = ab * ush + ub
            ab = ab * ash
        h = ab * carry + ub
        u_ref[pl.ds(off, SUBLANES), :] = h
        return jnp.broadcast_to(h[SUBLANES - 1:SUBLANES, :], (SUBLANES, LANES))

    carry = lax.fori_loop(0, length // SUBLANES, body, jnp.zeros((SUBLANES, LANES), F32), unroll=4)
    hlast_ref[0] = carry[0:1, :]
    aout_ref[0] = u_ref[...] * _gelu_tanh(yg_ref[0])


def _gate_blockdiag(w):
    z = jnp.zeros((4, LANES, LANES), F32)
    z = z.at[:, 0:HEAD_DIM, 0:HEAD_DIM].set(w[0::2])
    z = z.at[:, HEAD_DIM:, HEAD_DIM:].set(w[1::2])
    return z


def _lru_prompt(proj3, conv_w, conv_b, wr_bd, b_r, wi_bd, b_i, lam):
    batch, length, _ = proj3.shape
    n_ct = LRU_WIDTH // LANES
    vec = lambda b, c: (0, c)
    return pl.pallas_call(
        _lru_prompt_kernel,
        grid=(batch, n_ct),
        in_specs=[pl.BlockSpec((1, length, LANES), lambda b, c: (b, 0, c)),
                  pl.BlockSpec((1, length, LANES), lambda b, c: (b, 0, n_ct + c)),
                  pl.BlockSpec((CONV_W, LANES), vec), pl.BlockSpec((1, LANES), vec),
                  pl.BlockSpec((1, LANES, LANES), lambda b, c: (c, 0, 0)), pl.BlockSpec((1, LANES), vec),
                  pl.BlockSpec((1, LANES, LANES), lambda b, c: (c, 0, 0)), pl.BlockSpec((1, LANES), vec),
                  pl.BlockSpec((1, LANES), vec)],
        out_specs=[pl.BlockSpec((1, length, LANES), lambda b, c: (b, 0, c)),
                   pl.BlockSpec((1, 1, LANES), lambda b, c: (b, 0, c))],
        out_shape=[jax.ShapeDtypeStruct((batch, length, LRU_WIDTH), F32),
                   jax.ShapeDtypeStruct((batch, 1, LRU_WIDTH), F32)],
        scratch_shapes=[pltpu.VMEM((length + SUBLANES, LANES), F32), pltpu.VMEM((length, LANES), F32),
                        pltpu.VMEM((length, LANES), F32)],
        compiler_params=_cparams(2),
        name="lru_prompt",
    )(proj3, proj3, conv_w, conv_b.reshape(1, -1), wr_bd, b_r.reshape(1, -1), wi_bd, b_i.reshape(1, -1),
      lam.reshape(1, -1))


def _lru_decode_kernel(xr_ref, yg_ref, cbuf_ref, h0_ref, cw_ref, cb_ref, wr_ref, br_ref, wi_ref, bi_ref, lam_ref,
                       aout_ref, hlast_ref):
    steps = xr_ref.shape[0]
    xs = [cbuf_ref[j] for j in range(CONV_W - 1)] + [xr_ref[t] for t in range(steps)]
    cw = cw_ref[...]
    h = h0_ref[...]
    n_ct = LRU_WIDTH // LANES
    for t in range(steps):
        xc = cb_ref[...]
        for j in range(CONV_W):
            xc = xc + cw[j:j + 1, :] * xs[t + j]
        parts = []
        for c in range(n_ct):
            sl = slice(c * LANES, (c + 1) * LANES)
            parts.append(_lru_gates(xc[:, sl], wr_ref[c], br_ref[:, sl], wi_ref[c], bi_ref[:, sl], lam_ref[:, sl]))
        a = jnp.concatenate([p[0] for p in parts], axis=-1)
        u = jnp.concatenate([p[1] for p in parts], axis=-1)
        h = a * h + u
        aout_ref[t] = h * _gelu_tanh(yg_ref[t])
    hlast_ref[...] = h


def _lru_decode(proj3, cbuf, h0, conv_w, conv_b, wr_bd, b_r, wi_bd, b_i, lam):
    steps, batch, _ = proj3.shape
    full2 = lambda i: (0, 0)
    full3 = lambda i: (0, 0, 0)
    return pl.pallas_call(
        _lru_decode_kernel,
        grid=(1,),
        in_specs=[pl.BlockSpec((steps, batch, LRU_WIDTH), lambda i: (0, 0, 0)),
                  pl.BlockSpec((steps, batch, LRU_WIDTH), lambda i: (0, 0, 1)),
                  pl.BlockSpec(cbuf.shape, full3), pl.BlockSpec(h0.shape, full2),
                  pl.BlockSpec(conv_w.shape, full2), pl.BlockSpec((1, LRU_WIDTH), full2),
                  pl.BlockSpec(wr_bd.shape, full3), pl.BlockSpec((1, LRU_WIDTH), full2),
                  pl.BlockSpec(wi_bd.shape, full3), pl.BlockSpec((1, LRU_WIDTH), full2),
                  pl.BlockSpec((1, LRU_WIDTH), full2)],
        out_specs=[pl.BlockSpec((steps, batch, LRU_WIDTH), full3), pl.BlockSpec((batch, LRU_WIDTH), full2)],
        out_shape=[jax.ShapeDtypeStruct((steps, batch, LRU_WIDTH), F32),
                   jax.ShapeDtypeStruct((batch, LRU_WIDTH), F32)],
        compiler_params=_cparams(1),
        name="lru_decode",
    )(proj3, proj3, cbuf, h0, conv_w, conv_b.reshape(1, -1), wr_bd, b_r.reshape(1, -1), wi_bd,
      b_i.reshape(1, -1), lam.reshape(1, -1))


def _decode_bias(slopes_hq, q_times, window, dil, steps):
    heads, nq = slopes_hq.shape
    bias = np.zeros((heads, nq, window + LANES), np.float64)
    w = np.arange(window)
    for qi, t in enumerate(q_times):
        if t < 0:
            continue
        dist = window + t - w
        ok = (dist % dil == 0) & (dist <= window)
        bias[:, qi, :window] = np.where(ok[None], -slopes_hq[:, qi, None] * dist[None], -np.inf)
        for c in range(LANES):
            d = t - c
            good = c < steps and d >= 0 and d % dil == 0
            bias[:, qi, window + c] = -slopes_hq[:, qi] * d if good else -np.inf
    return jnp.asarray(bias.astype(np.float32))


def _decode_kernel(*refs, window, steps, has_sink, has_alias):
    refs = list(refs)
    q_ref, c_ref, new_ref, bias_ref = refs[:4]
    pos = 4
    sink_ref = None
    if has_sink:
        sink_ref = refs[pos]
        pos += 1
    if has_alias:
        pos += 1
    co_ref, o_ref, lse_ref = refs[pos:pos + 3]
    c = c_ref[0]
    bb, _, hb, hd, _ = c.shape
    nq = q_ref.shape[2]
    trow = lax.broadcasted_iota(jnp.int32, (SUBLANES, LANES), 0)
    tlane = lax.broadcasted_iota(jnp.int32, (SUBLANES, LANES), 1)
    place = jnp.where((trow == tlane) & (trow < steps), 1.0, 0.0).astype(F32)
    place = jnp.broadcast_to(place[None], (bb * 2 * hb, SUBLANES, LANES))
    new = jnp.einsum("htd,htl->hdl", new_ref[...].reshape(bb * 2 * hb, SUBLANES, hd), place,
                     preferred_element_type=F32, precision=lax.Precision.HIGHEST).reshape(bb, 2, hb, hd, LANES)
    merge = lambda a: a.reshape((bb * hb,) + a.shape[2:])
    k_ext = jnp.concatenate([merge(c[:, 0]), merge(new[:, 0])], axis=-1).astype(BF16)
    v_ext = jnp.concatenate([merge(c[:, 1]), merge(new[:, 1])], axis=-1).astype(BF16)
    q = merge(q_ref[...]).astype(BF16)
    bias = merge(jnp.broadcast_to(bias_ref[...][None], (bb,) + bias_ref.shape))
    s = jnp.einsum("hqd,hdw->hqw", q, k_ext, preferred_element_type=F32) * (HEAD_DIM ** -0.5) + bias
    m = jnp.max(s, axis=-1, keepdims=True)
    e = jnp.exp(s - m)
    l = jnp.sum(e, axis=-1, keepdims=True)
    lse = m + jnp.log(l)
    p = e / l
    if has_sink:
        sink = merge(jnp.broadcast_to(sink_ref[...][None], (bb,) + sink_ref.shape))
        p = p * jax.nn.sigmoid(lse - sink[:, :, 0:1])
    o = jnp.einsum("hqw,hdw->hqd", p.astype(BF16), v_ext, preferred_element_type=F32)
    o_ref[...] = o.reshape(bb, hb, nq, hd)
    lse_ref[...] = jnp.broadcast_to(lse, (bb * hb, nq, LANES)).reshape(bb, hb, nq, LANES)
    co_ref[0] = pltpu.roll(c, window - steps, 4)
    lane = lax.broadcasted_iota(jnp.int32, new.shape, 4)
    last = pltpu.roll(c[:, :, :, :, window - LANES:], LANES - steps, 4)
    tail = pltpu.roll(new, LANES - steps, 4)
    co_ref[0, :, :, :, :, window - LANES:] = jnp.where(lane >= LANES - steps, tail, last)


def _decode_attention(layer, cache_t, prev_out, q, new_front, bias, sink, rows_per_step, heads_per_step,
                      steps, name):
    n_layers, batch, _, heads, hd, window = cache_t.shape
    nq = q.shape[2]
    bb, hb = rows_per_step, heads_per_step
    cmap = lambda b, h: (layer, b, 0, h, 0, 0)
    in_specs = [pl.BlockSpec((bb, hb, nq, hd), lambda b, h: (b, h, 0, 0)),
                pl.BlockSpec((1, bb, 2, hb, hd, window), cmap),
                pl.BlockSpec((bb, 2, hb, SUBLANES, hd), lambda b, h: (b, 0, h, 0, 0)),
                pl.BlockSpec((hb, nq, window + LANES), lambda b, h: (h, 0, 0))]
    args = [q, cache_t, new_front, bias]
    if sink is not None:
        in_specs.append(pl.BlockSpec((hb, nq, LANES), lambda b, h: (h, 0, 0)))
        args.append(sink)
    aliases = {}
    if prev_out is not None:
        in_specs.append(pl.BlockSpec(memory_space=pl.ANY))
        aliases = {len(args): 0}
        args.append(prev_out)
    return pl.pallas_call(
        functools.partial(_decode_kernel, window=window, steps=steps, has_sink=sink is not None,
                          has_alias=prev_out is not None),
        grid=(batch // bb, heads // hb),
        in_specs=in_specs,
        out_specs=[pl.BlockSpec((1, bb, 2, hb, hd, window), cmap),
                   pl.BlockSpec((bb, hb, nq, hd), lambda b, h: (b, h, 0, 0)),
                   pl.BlockSpec((bb, hb, nq, LANES), lambda b, h: (b, h, 0, 0))],
        out_shape=[jax.ShapeDtypeStruct(cache_t.shape, F32),
                   jax.ShapeDtypeStruct((batch, heads, nq, hd), F32),
                   jax.ShapeDtypeStruct((batch, heads, nq, LANES), F32)],
        input_output_aliases=aliases,
        compiler_params=_cparams(2),
        name=name,
    )(*args)


def _router_kernel(x_ref, wt_ref, b_ref, e_ref, g_ref, c_ref):
    logits = lax.dot_general(wt_ref[...].astype(BF16), x_ref[...].astype(BF16), (((1,), (1,)), ((), ())),
                             preferred_element_type=F32)
    scores = jax.nn.sigmoid(logits)
    biased = scores + b_ref[...]
    tm = biased.shape[1]
    b3 = biased.reshape(N_EXPERT_GROUPS, GROUP_SIZE, tm)
    i3 = lax.broadcasted_iota(jnp.int32, b3.shape, 1).astype(F32)
    g1 = jnp.max(b3, axis=1, keepdims=True)
    first = jnp.min(jnp.where(b3 == g1, i3, float(GROUP_SIZE)), axis=1, keepdims=True)
    g2 = jnp.max(jnp.where(i3 == first, NEG_INF, b3), axis=1, keepdims=True)
    gs = g1 + g2
    gi = lax.broadcasted_iota(jnp.int32, gs.shape, 0).astype(F32)
    sel = jnp.zeros(gs.shape, F32)
    for _ in range(TOPK_GROUPS):
        m = jnp.max(gs, axis=0, keepdims=True)
        f = jnp.min(jnp.where(gs == m, gi, float(N_EXPERT_GROUPS)), axis=0, keepdims=True)
        hit = gi == f
        sel = jnp.where(hit, 1.0, sel)
        gs = jnp.where(hit, NEG_INF, gs)
    masked = jnp.where(sel > 0.5, b3, NEG_INF).reshape(N_EXPERTS, tm)
    ei = lax.broadcasted_iota(jnp.int32, masked.shape, 0).astype(F32)
    e_rows, g_rows = [], []
    chosen = jnp.zeros(masked.shape, F32)
    for _ in range(TOP_K):
        m = jnp.max(masked, axis=0, keepdims=True)
        f = jnp.min(jnp.where(masked == m, ei, float(N_EXPERTS)), axis=0, keepdims=True)
        hit = ei == f
        g_rows.append(jnp.sum(jnp.where(hit, scores, 0.0), axis=0, keepdims=True))
        e_rows.append(f)
        chosen = jnp.where(hit, 1.0, chosen)
        masked = jnp.where(hit, NEG_INF, masked)
    gates = jnp.concatenate(g_rows, axis=0)
    gates = gates / jnp.sum(gates, axis=0, keepdims=True) * ROUTED_SCALE
    e_ref[...] = jnp.concatenate(e_rows, axis=0).astype(jnp.int32)
    g_ref[...] = gates
    c_ref[0] = jnp.sum(chosen, axis=1, keepdims=True)


def _router(x, w_router_t, bias, tm):
    n = x.shape[0]
    nt = n // tm
    return pl.pallas_call(
        _router_kernel,
        grid=(nt,),
        in_specs=[pl.BlockSpec((tm, D_MODEL), lambda i: (i, 0)), pl.BlockSpec((N_EXPERTS, D_MODEL), lambda i: (0, 0)),
                  pl.BlockSpec((N_EXPERTS, 1), lambda i: (0, 0))],
        out_specs=[pl.BlockSpec((TOP_K, tm), lambda i: (0, i)), pl.BlockSpec((TOP_K, tm), lambda i: (0, i)),
                   pl.BlockSpec((1, N_EXPERTS, 1), lambda i: (i, 0, 0))],
        out_shape=[jax.ShapeDtypeStruct((TOP_K, n), jnp.int32), jax.ShapeDtypeStruct((TOP_K, n), F32),
                   jax.ShapeDtypeStruct((nt, N_EXPERTS, 1), F32)],
        compiler_params=_cparams(1),
        name="moe_router",
    )(x, w_router_t, bias.reshape(N_EXPERTS, 1))


KEY_EXPERT_SHIFT = 18
KEY_PAD_BIT = 17


def _dispatch_plan(top_e, gates, tile_counts, chunk, tm):
    n = top_e.shape[1]
    n_chunks = n // chunk
    pairs = TOP_K * chunk
    assert pairs < (1 << KEY_PAD_BIT) and tm <= (1 << KEY_PAD_BIT) and pairs % tm == 0
    per_chunk = lambda a: jnp.transpose(a.reshape(TOP_K, n_chunks, chunk), (1, 0, 2)).reshape(n_chunks, pairs)
    te, ga = per_chunk(top_e), per_chunk(gates)
    counts = tile_counts.reshape(n_chunks, -1, N_EXPERTS).sum(axis=1).astype(jnp.int32)
    n_pad = (-counts) % tm
    key_real = (te << KEY_EXPERT_SHIFT) + jnp.arange(pairs, dtype=jnp.int32)[None, :]
    e_pad = jnp.arange(N_EXPERTS, dtype=jnp.int32)[None, :, None]
    j_pad = jnp.arange(tm, dtype=jnp.int32)[None, None, :]
    key_pad = jnp.where(j_pad < n_pad[:, :, None],
                        (e_pad << KEY_EXPERT_SHIFT) + (1 << KEY_PAD_BIT) + j_pad,
                        (N_EXPERTS << KEY_EXPERT_SHIFT) + e_pad * tm + j_pad)
    keys = jnp.concatenate([key_real, key_pad.reshape(n_chunks, N_EXPERTS * tm)], axis=1)
    vals = jnp.concatenate([ga, jnp.zeros((n_chunks, N_EXPERTS * tm), F32)], axis=1)
    keys, vals = lax.sort((keys, vals), dimension=1, num_keys=1, is_stable=False)
    e_row = keys >> KEY_EXPERT_SHIFT
    real = ((keys >> KEY_PAD_BIT) & 1) == 0
    real = real & (e_row < N_EXPERTS)
    tok = (keys & ((1 << KEY_PAD_BIT) - 1)) % chunk
    nb = pairs // tm + N_EXPERTS
    gather_idx = jnp.where(real, tok, 0).reshape(n_chunks * nb, 1, tm)
    scatter_idx = jnp.where(real, tok, chunk).reshape(n_chunks * nb, 1, tm)
    scatter_idx = jnp.concatenate([scatter_idx, jnp.full((1, 1, tm), chunk, jnp.int32)], axis=0)
    block_e = jnp.minimum(e_row[:, ::tm], N_EXPERTS - 1).reshape(n_chunks * nb)
    used = ((counts + n_pad).sum(axis=1) // tm).astype(jnp.int32)
    return gather_idx, scatter_idx, vals.reshape(n_chunks * nb * tm, 1), block_e, used, nb


SCATTER_BATCH = 16


def _expert_kernel(be_ref, used_ref, gcur_ref, gnext_ref, scur_ref, sprev_ref, gate_ref, x_hbm, wg_ref, wu_ref,
                   wd_ref, y_hbm, x_s, y_s, xg0, xg1, ob0, ob1, sem, *, tm, chunk):
    c = pl.program_id(0)
    j = pl.program_id(1)
    slot = j % 2
    used = used_ref[c]
    rows = chunk * ROW_TILES
    xg = (xg0, xg1)
    obuf = (ob0, ob1)

    def gather(idx_ref, dst):
        for r in range(tm):
            t = pl.multiple_of(idx_ref[0, 0, r] * ROW_TILES, ROW_TILES)
            dst[r * ROW_TILES:(r + 1) * ROW_TILES, :] = x_s[pl.ds(t, ROW_TILES), :]

    def scatter_add(idx_ref, src):
        for r0 in range(0, tm, SCATTER_BATCH):
            offs = [pl.multiple_of(idx_ref[0, 0, r0 + i] * ROW_TILES, ROW_TILES) for i in range(SCATTER_BATCH)]
            sums = [y_s[pl.ds(offs[i], ROW_TILES), :] + src[(r0 + i) * ROW_TILES:(r0 + i + 1) * ROW_TILES, :]
                    for i in range(SCATTER_BATCH)]
            for i in range(SCATTER_BATCH):
                y_s[pl.ds(offs[i], ROW_TILES), :] = sums[i]

    @pl.when(j == 0)
    def _():
        load = pltpu.make_async_copy(x_hbm.at[pl.ds(c * rows, rows)], x_s, sem.at[0])
        load.start()
        y_s[...] = jnp.zeros(y_s.shape, F32)

        @pl.when(c == 0)
        def _():
            ob0[...] = jnp.zeros(ob0.shape, F32)
            ob1[...] = jnp.zeros(ob1.shape, F32)

        load.wait()
        gather(gcur_ref, xg0)

    def block(cur, nxt):
        gather(gnext_ref, xg[nxt])
        x = _load_tile_rows(xg[cur], tm).astype(BF16)
        hg = jnp.dot(x, wg_ref[0, 0].astype(BF16), preferred_element_type=F32)
        hu = jnp.dot(x, wu_ref[0, 0].astype(BF16), preferred_element_type=F32)
        hid = (hg * jax.nn.sigmoid(hg)) * hu
        out = jnp.dot(hid.astype(BF16), wd_ref[0, 0].astype(BF16), preferred_element_type=F32) * gate_ref[...]
        _store_tile_rows(obuf[cur], out, tm)
        scatter_add(sprev_ref, obuf[nxt])

    for s in (0, 1):
        @pl.when((j < used) & (slot == s))
        def _(s=s):
            block(s, 1 - s)

    for s in (0, 1):
        @pl.when((j == used - 1) & (slot == s))
        def _(s=s):
            scatter_add(scur_ref, obuf[s])
            store = pltpu.make_async_copy(y_s.at[pl.ds(0, rows)], y_hbm.at[pl.ds(c * rows, rows)], sem.at[1])
            store.start()
            store.wait()


def _experts(x_tiles, plan, layer, w_gate, w_up, w_down, n_tokens, chunk, tm):
    gather_idx, scatter_idx, row_gate, block_e, used, nb = plan
    n_chunks = n_tokens // chunk
    dump_block = n_chunks * nb
    smem_blk = lambda f: pl.BlockSpec((1, 1, tm), f, memory_space=pltpu.SMEM)
    wmap = lambda c, j, be, us: (layer, be[c * nb + j], 0, 0)
    grid_spec = pltpu.PrefetchScalarGridSpec(
        num_scalar_prefetch=2,
        grid=(n_chunks, nb),
        in_specs=[smem_blk(lambda c, j, be, us: (c * nb + j, 0, 0)),
                  smem_blk(lambda c, j, be, us: (c * nb + jnp.minimum(j + 1, nb - 1), 0, 0)),
                  smem_blk(lambda c, j, be, us: (c * nb + j, 0, 0)),
                  smem_blk(lambda c, j, be, us: (jnp.where(j == 0, dump_block, c * nb + j - 1), 0, 0)),
                  pl.BlockSpec((tm, 1), lambda c, j, be, us: (c * nb + j, 0)),
                  pl.BlockSpec(memory_space=pl.ANY),
                  pl.BlockSpec((1, 1, D_MODEL, D_EXPERT), wmap), pl.BlockSpec((1, 1, D_MODEL, D_EXPERT), wmap),
                  pl.BlockSpec((1, 1, D_EXPERT, D_MODEL), wmap)],
        out_specs=pl.BlockSpec(memory_space=pl.ANY),
        scratch_shapes=[pltpu.VMEM((chunk * ROW_TILES, LANES), F32),
                        pltpu.VMEM(((chunk + 1) * ROW_TILES, LANES), F32),
                        pltpu.VMEM((tm * ROW_TILES, LANES), F32), pltpu.VMEM((tm * ROW_TILES, LANES), F32),
                        pltpu.VMEM((tm * ROW_TILES, LANES), F32), pltpu.VMEM((tm * ROW_TILES, LANES), F32),
                        pltpu.SemaphoreType.DMA((2,))],
    )
    return pl.pallas_call(
        functools.partial(_expert_kernel, tm=tm, chunk=chunk),
        grid_spec=grid_spec,
        out_shape=jax.ShapeDtypeStruct((n_tokens * ROW_TILES, LANES), F32),
        compiler_params=_cparams(2),
        name="moe_experts",
    )(block_e, used, gather_idx, gather_idx, scatter_idx, scatter_idx, row_gate, x_tiles, w_gate, w_up, w_down)


def _combine_kernel(y_ref, x_ref, wsg_ref, wsu_ref, wsd_ref, g_ref, b_ref, o_ref, wsg_bf, wsu_bf, wsd_bf, *, tm):
    @pl.when(pl.program_id(0) == 0)
    def _():
        wsg_bf[...] = wsg_ref[...].astype(BF16)
        wsu_bf[...] = wsu_ref[...].astype(BF16)
        wsd_bf[...] = wsd_ref[...].astype(BF16)

    routed = _load_tile_rows(y_ref, tm)
    x = x_ref[...]
    xb = x.astype(BF16)
    sg = jnp.dot(xb, wsg_bf[...], preferred_element_type=F32)
    su = jnp.dot(xb, wsu_bf[...], preferred_element_type=F32)
    shared = jnp.dot(((sg * jax.nn.sigmoid(sg)) * su).astype(BF16), wsd_bf[...], preferred_element_type=F32)
    o_ref[...] = _layer_norm(DEEPNORM_ALPHA * x + (routed + shared), g_ref[...], b_ref[...])


def _combine(y_routed, x, ws_gate, ws_up, ws_down, g, b, tm):
    n = x.shape[0]
    fixed = lambda i: (0, 0)
    return pl.pallas_call(
        functools.partial(_combine_kernel, tm=tm),
        grid=(n // tm,),
        in_specs=[pl.BlockSpec((tm * ROW_TILES, LANES), lambda i: (i, 0)), pl.BlockSpec((tm, D_MODEL), lambda i: (i, 0)),
                  pl.BlockSpec(ws_gate.shape, fixed), pl.BlockSpec(ws_up.shape, fixed),
                  pl.BlockSpec(ws_down.shape, fixed),
                  pl.BlockSpec((1, D_MODEL), fixed), pl.BlockSpec((1, D_MODEL), fixed)],
        out_specs=pl.BlockSpec((tm, D_MODEL), lambda i: (i, 0)),
        out_shape=jax.ShapeDtypeStruct((n, D_MODEL), F32),
        scratch_shapes=[pltpu.VMEM(ws_gate.shape, BF16), pltpu.VMEM(ws_up.shape, BF16),
                        pltpu.VMEM(ws_down.shape, BF16)],
        compiler_params=_cparams(1),
        name="moe_combine",
    )(y_routed, x, ws_gate, ws_up, ws_down, g.reshape(1, -1), b.reshape(1, -1))


def _moe_block(x, x_tiles, layer, w_router_t, router_bias, w_gate, w_up, w_down, ws_gate, ws_up, ws_down, g, b,
               tm_route, chunk, tm_expert, tm_combine):
    n = x.shape[0]
    top_e, gates, tile_counts = _router(x, w_router_t, router_bias, tm_route)
    plan = _dispatch_plan(top_e, gates, tile_counts, chunk, tm_expert)
    y_routed = _experts(x_tiles, plan, layer, w_gate, w_up, w_down, n, chunk, tm_expert)
    return _combine(y_routed, x, ws_gate, ws_up, ws_down, g, b, tm_combine)


def _to_cache_t(cache):
    return jnp.transpose(cache, (0, 1, 3, 4, 5, 2))


def _from_cache_t(cache_t):
    return jnp.transpose(cache_t, (0, 1, 5, 2, 3, 4))


def _new_kv_columns(k, v, steps, batch, heads):
    kv = jnp.stack([k, v]).reshape(2, steps, batch, heads, HEAD_DIM)
    kv = jnp.transpose(kv, (2, 0, 3, 1, 4))
    return jnp.pad(kv, ((0, 0),) * 3 + ((0, SUBLANES - steps), (0, 0)))


def kernel(x_prompt, x_sample, state_lru_h, state_lru_conv, cache_swa_kv, cache_dil1_kv, cache_dil2_kv,
           cache_dil3_kv, w_in_even, conv_w, conv_b, w_rgate, b_rgate, w_igate, b_igate, lru_lambda, swa_sinks,
           w_out_even, w_in_odd, w_out_odd, ln1_g, ln1_b, ln2_g, ln2_b, w_router, router_bias, w_exp_gate,
           w_exp_up, w_exp_down, w_sh_gate, w_sh_up, w_sh_down):
    batch, seq, _ = x_prompt.shape
    dbatch, steps, _ = x_sample.shape
    n_p = batch * seq
    n_s = dbatch * steps
    xp = x_prompt.reshape(n_p, D_MODEL)
    xs = jnp.transpose(x_sample, (1, 0, 2)).reshape(n_s, D_MODEL)

    slopes8 = _alibi_slopes(SWA_HEADS)
    pad_q = SUBLANES - steps
    q_times_dil = list(range(steps)) + [-1] * pad_q
    swa_q_times = [t for t in range(steps) for _ in range(SWA_GROUP)]
    swa_slopes = np.stack([np.tile(slopes8[kv * SWA_GROUP:(kv + 1) * SWA_GROUP], steps)
                           for kv in range(SWA_KV_HEADS)])
    swa_bias = _decode_bias(swa_slopes, swa_q_times, SWA_WINDOW, 1, steps)
    dil_bias = [_decode_bias(np.tile(_alibi_slopes(DIL_HEADS)[:, None], (1, SUBLANES)), q_times_dil,
                             DIL_WINDOWS[g], DIL_RATES[g], steps) for g in range(N_DIL)]

    et_flat = _head_expand_matrix(DIL_HEADS, DIL_HEADS)
    et_fold = _head_expand_matrix(DIL_HEADS, FOLD_HEADS)
    ets_prompt = [et_flat if d == 1 else et_fold for d in DIL_RATES]
    ets_sample = [et_flat] * N_DIL

    swa_t = _to_cache_t(cache_swa_kv)
    dil_t = [_to_cache_t(c) for c in (cache_dil1_kv, cache_dil2_kv, cache_dil3_kv)]
    swa_out = None
    dil_out = [None] * N_DIL
    lru_h_p, lru_h_s, lru_c_p, lru_c_s, swa_p = [], [], [], [], []
    dil_p = [[] for _ in range(N_DIL)]

    for l in range(DEPTH):
        j = l // 2
        if l % 2 == 0:
            wr_bd, wi_bd = _gate_blockdiag(w_rgate[j]), _gate_blockdiag(w_igate[j])
            proj = _matmul(xp, w_in_even[j], 1024, 896, "inproj_even_prompt")
            proj3 = proj.reshape(batch, seq, -1)
            a_out, h_last = _lru_prompt(proj3, conv_w[j], conv_b[j], wr_bd, b_rgate[j], wi_bd, b_igate[j],
                                        lru_lambda[j])
            (o_swa,) = _banded_attention(proj3, batch, seq, 1, (0, 2), (0, 12), (0, 13), SWA_KV_HEADS * HEAD_DIM,
                                         SWA_HEADS, SWA_GROUP, slopes8, 1, swa_sinks[j], "swa_prompt")
            lru_h_p.append(h_last[:, 0])
            lru_c_p.append(proj3[:, seq - (CONV_W - 1):, :LRU_WIDTH])
            keep = min(SWA_WINDOW, seq)
            swa_p.append(proj3[:, seq - keep:, 2 * LRU_WIDTH + SWA_HEADS * HEAD_DIM:]
                         .reshape(batch, keep, 2, SWA_KV_HEADS, HEAD_DIM))
            xp, xp_tiles = _outproj_even(xp, a_out.reshape(n_p, -1), o_swa.reshape(n_p, -1), w_out_even[j],
                                         ln1_g[l], ln1_b[l], 512)
            proj_s = _matmul(xs, w_in_even[j], n_s, 896, "inproj_even_sample")
            proj_s3 = proj_s.reshape(steps, dbatch, -1)
            cbuf = jnp.transpose(state_lru_conv[j], (1, 0, 2))
            a_s, h_s = _lru_decode(proj_s3, cbuf, state_lru_h[j], conv_w[j], conv_b[j], wr_bd, b_rgate[j], wi_bd,
                                   b_igate[j], lru_lambda[j])
            lru_h_s.append(h_s)
            xr_s = proj_s3[:, :, :LRU_WIDTH]
            lru_c_s.append(jnp.transpose(jnp.concatenate([cbuf, xr_s], axis=0)[-(CONV_W - 1):], (1, 0, 2)))
            o0 = 2 * LRU_WIDTH
            q_s = proj_s[:, o0:o0 + SWA_HEADS * HEAD_DIM].reshape(steps, dbatch, SWA_KV_HEADS, SWA_GROUP, HEAD_DIM)
            q_s = jnp.transpose(q_s, (1, 2, 0, 3, 4)).reshape(dbatch, SWA_KV_HEADS, steps * SWA_GROUP, HEAD_DIM)
            o1 = o0 + SWA_HEADS * HEAD_DIM
            o2 = o1 + SWA_KV_HEADS * HEAD_DIM
            front = _new_kv_columns(proj_s[:, o1:o2], proj_s[:, o2:], steps, dbatch, SWA_KV_HEADS)
            sink = jnp.broadcast_to(
                jnp.tile(swa_sinks[j].reshape(SWA_KV_HEADS, SWA_GROUP), (1, steps))[:, :, None],
                (SWA_KV_HEADS, steps * SWA_GROUP, LANES))
            swa_out, o_dec, _ = _decode_attention(j, swa_t, swa_out, q_s, front, swa_bias, sink,
                                                  min(dbatch, 16), SWA_KV_HEADS, steps, "swa_decode")
            o_dec = o_dec.reshape(dbatch, SWA_KV_HEADS, steps, SWA_GROUP, HEAD_DIM)
            o_dec = jnp.transpose(o_dec, (2, 0, 1, 3, 4)).reshape(n_s, SWA_HEADS * HEAD_DIM)
            xs, xs_tiles = _outproj_even(xs, a_s.reshape(n_s, -1), o_dec, w_out_even[j], ln1_g[l], ln1_b[l], n_s)
        else:
            proj = _matmul(xp, w_in_odd[j], 1024, 768, "inproj_odd_prompt")
            row_w = proj.shape[1]
            proj3 = proj.reshape(batch, seq, row_w)
            os_, ls_ = [], []
            for g in range(N_DIL):
                dil = DIL_RATES[g]
                if dil == 1:
                    n_blk = row_w // (DIL_HEADS * HEAD_DIM)
                    o_g, lse_g = _banded_attention(proj3, batch, seq, 1, (n_blk, 3 * g), (n_blk, 3 * g + 1),
                                                   (n_blk, 3 * g + 2), DIL_HEADS * HEAD_DIM, DIL_HEADS, 1,
                                                   _alibi_slopes(DIL_HEADS), dil, None, "dil_prompt")
                else:
                    o_g, lse_g = _fold_attention(proj3, g, dil, _alibi_slopes(DIL_HEADS))
                os_.append(o_g.reshape(n_p, -1))
                ls_.append(lse_g.reshape(n_p, -1))
                keep = min(DIL_WINDOWS[g], seq)
                c0 = g * 3 * DIL_HEADS * HEAD_DIM + DIL_HEADS * HEAD_DIM
                dil_p[g].append(proj3[:, seq - keep:, c0:c0 + 2 * DIL_HEADS * HEAD_DIM]
                                .reshape(batch, keep, 2, DIL_HEADS, HEAD_DIM))
            xp, xp_tiles = _outproj_odd(xp, os_, ls_, ets_prompt, w_out_odd[j], ln1_g[l], ln1_b[l], 512)
            proj_s = _matmul(xs, w_in_odd[j], n_s, 768, "inproj_odd_sample")
            os_, ls_ = [], []
            gw = DIL_HEADS * HEAD_DIM
            for g in range(N_DIL):
                c0 = g * 3 * gw
                q_s = proj_s[:, c0:c0 + gw].reshape(steps, dbatch, DIL_HEADS, HEAD_DIM)
                q_s = jnp.pad(jnp.transpose(q_s, (1, 2, 0, 3)), ((0, 0), (0, 0), (0, pad_q), (0, 0)))
                front = _new_kv_columns(proj_s[:, c0 + gw:c0 + 2 * gw], proj_s[:, c0 + 2 * gw:c0 + 3 * gw],
                                        steps, dbatch, DIL_HEADS)
                rows_per_step, heads_per_step = ((8, 8), (2, 8), (1, 4))[g]
                dil_out[g], o_dec, lse_dec = _decode_attention(j, dil_t[g], dil_out[g], q_s, front,
                                                               dil_bias[g], None, min(dbatch, rows_per_step),
                                                               heads_per_step, steps, "dil_decode")
                o_dec = jnp.transpose(o_dec[:, :, :steps], (2, 0, 1, 3)).reshape(n_s, gw)
                lse_dec = jnp.transpose(lse_dec[:, :, :steps, 0], (2, 0, 1)).reshape(n_s, DIL_HEADS)
                os_.append(o_dec)
                ls_.append(jnp.pad(lse_dec, ((0, 0), (0, LANES - DIL_HEADS))))
            xs, xs_tiles = _outproj_odd(xs, os_, ls_, ets_sample, w_out_odd[j], ln1_g[l], ln1_b[l], n_s)

        moe_w = (l, jnp.transpose(w_router[l]), router_bias[l], w_exp_gate, w_exp_up, w_exp_down,
                 w_sh_gate[l], w_sh_up[l], w_sh_down[l], ln2_g[l], ln2_b[l])
        xp = _moe_block(xp, xp_tiles, *moe_w, tm_route=512, chunk=min(n_p, 4096), tm_expert=256, tm_combine=512)
        xs = _moe_block(xs, xs_tiles, *moe_w, tm_route=n_s, chunk=n_s, tm_expert=128, tm_combine=n_s)

    y_prompt = xp.reshape(batch, seq, D_MODEL)
    y_sample = jnp.transpose(xs.reshape(steps, dbatch, D_MODEL), (1, 0, 2))
    return (y_prompt, y_sample, jnp.stack(lru_h_p), jnp.stack(lru_h_s), jnp.stack(lru_c_p), jnp.stack(lru_c_s),
            jnp.stack(swa_p), _from_cache_t(swa_out),
            jnp.stack(dil_p[0]), _from_cache_t(dil_out[0]), jnp.stack(dil_p[1]), _from_cache_t(dil_out[1]),
            jnp.stack(dil_p[2]), _from_cache_t(dil_out[2]))
```
